```python
import jax, jax.numpy as jnp
from jax import lax
import numpy as np

D_MODEL = 2048
BATCH = 4
SEQ = 2048
DEPTH = 2

POOL_WIDTH = 1024
POOL_GROUPS = 4
POOL_WINDOWS = (2, 4, 8, 16)
POOL_GROUP_DIM = POOL_WIDTH // POOL_GROUPS
CONF_WIDTH = 1024
CONF_KERNEL = 31
SCONV_WIDTH = 1024
SCONV_KERNEL = 3
N_BRANCH = 3
OFF_POOL = POOL_WIDTH
OFF_CONF = OFF_POOL + 2 * CONF_WIDTH
OFF_SCONV = OFF_CONF + 3 * SCONV_WIDTH
D_IN = OFF_SCONV + N_BRANCH * D_MODEL
D_FF = 5504
FFN_KERNEL = 3
EPS = 1e-6

kernel_name = 'hybrid_pool_conformer_shortconv_block'


def rms_norm(x, g):
    xf = x.astype(jnp.float32)
    y = xf * lax.rsqrt(jnp.mean(xf * xf, axis=-1, keepdims=True) + EPS)
    return (y * g.astype(jnp.float32)).astype(x.dtype)


def layer_norm(x, g, b):
    xf = x.astype(jnp.float32)
    mu = jnp.mean(xf, axis=-1, keepdims=True)
    xc = xf - mu
    var = jnp.mean(xc * xc, axis=-1, keepdims=True)
    y = xc * lax.rsqrt(var + EPS) * g.astype(jnp.float32) + b.astype(jnp.float32)
    return y.astype(x.dtype)


def causal_dwconv(x, w):
    k, c = w.shape
    return lax.conv_general_dilated(
        x, w[:, None, :].astype(x.dtype), window_strides=(1,), padding=[(k - 1, 0)],
        dimension_numbers=('NWC', 'WIO', 'NWC'), feature_group_count=c)


def multiscale_pool(u, w_grp, scale):
    b, s, _ = u.shape
    ug = u.reshape(b, s, POOL_GROUPS, POOL_GROUP_DIM).astype(jnp.float32)
    cs0 = jnp.pad(jnp.cumsum(ug, axis=1), ((0, 0), (1, 0), (0, 0), (0, 0)))
    pos = jnp.arange(1, s + 1, dtype=jnp.float32)
    outs = []
    for g, w in enumerate(POOL_WINDOWS):
        cg = cs0[:, :, g]
        lagged = jnp.pad(cg[:, :s + 1 - w], ((0, 0), (w, 0), (0, 0)))
        window_sum = cg[:, 1:] - lagged[:, 1:]
        count = jnp.minimum(pos, float(w))[None, :, None]
        outs.append(window_sum / count - ug[:, :, g])
    pooled = jnp.stack(outs, axis=2).astype(u.dtype)
    mixed = jnp.einsum('bsgc,gcd->bsgd', pooled, w_grp)
    return mixed.reshape(b, s, POOL_WIDTH) * scale


def setup_inputs(seed: int = 0) -> dict:
    key = jax.random.key(seed)
    ks = jax.random.split(key, 24)
    L, D = DEPTH, D_MODEL
    nrm = lambda k, shape, fan: jax.random.normal(k, shape, jnp.float32) * (fan ** -0.5)
    gain = lambda k, shape: 1.0 + 0.05 * jax.random.normal(k, shape, jnp.float32)
    small = lambda k, shape: 0.02 * jax.random.normal(k, shape, jnp.float32)
    return {
        'x': jax.random.normal(ks[0], (BATCH, SEQ, D), jnp.float32),
        'norm1_g': gain(ks[1], (L, D)),
        'w_in': nrm(ks[2], (L, D, D_IN), D),
        'gate_b': small(ks[3], (L, N_BRANCH * D)),
        'pool_w': nrm(ks[4], (L, POOL_GROUPS, POOL_GROUP_DIM, POOL_GROUP_DIM), POOL_GROUP_DIM),
        'pool_scale': gain(ks[5], (L, POOL_WIDTH)),
        'pool_proj': nrm(ks[6], (L, POOL_WIDTH, D), POOL_WIDTH),
        'conf_conv_w': nrm(ks[7], (L, CONF_KERNEL, CONF_WIDTH), CONF_KERNEL),
        'conf_conv_b': small(ks[8], (L, CONF_WIDTH)),
        'conf_ln_g': gain(ks[9], (L, CONF_WIDTH)),
        'conf_ln_b': small(ks[10], (L, CONF_WIDTH)),
        'conf_proj': nrm(ks[11], (L, CONF_WIDTH, D), CONF_WIDTH),
        'sconv_w': nrm(ks[12], (L, SCONV_KERNEL, SCONV_WIDTH), SCONV_KERNEL),
        'sconv_proj': nrm(ks[13], (L, SCONV_WIDTH, D), SCONV_WIDTH),
        'w_o': nrm(ks[14], (L, D, D), D),
        'norm2_g': gain(ks[15], (L, D)),
        'ffn_up': nrm(ks[16], (L, D, 2 * D_FF), D),
        'ffn_conv_w': nrm(ks[17], (L, FFN_KERNEL, 2 * D_FF), FFN_KERNEL),
        'ffn_down': nrm(ks[18], (L, D_FF, D), D_FF),
        'final_g': gain(ks[19], (D,)),
    }


def reference(x, norm1_g, w_in, gate_b, pool_w, pool_scale, pool_proj, conf_conv_w, conf_conv_b,
              conf_ln_g, conf_ln_b, conf_proj, sconv_w, sconv_proj, w_o, norm2_g, ffn_up, ffn_conv_w,
              ffn_down, final_g):
    for l in range(DEPTH):
        h = rms_norm(x, norm1_g[l])
        z = h @ w_in[l]
        u_pool, u_conf, u_sc, gate_logits = jnp.split(z, [OFF_POOL, OFF_CONF, OFF_SCONV], axis=-1)
        br_a = multiscale_pool(u_pool, pool_w[l], pool_scale[l]) @ pool_proj[l]
        val, gte = jnp.split(u_conf, 2, axis=-1)
        v = val * jax.nn.sigmoid(gte)
        v = causal_dwconv(v, conf_conv_w[l]) + conf_conv_b[l]
        v = jax.nn.silu(layer_norm(v, conf_ln_g[l], conf_ln_b[l]))
        br_b = v @ conf_proj[l]
        bg, cg, hs = jnp.split(u_sc, 3, axis=-1)
        br_c = (bg * causal_dwconv(cg * hs, sconv_w[l])) @ sconv_proj[l]
        ga, gb, gc = jnp.split(jax.nn.sigmoid(gate_logits + gate_b[l]), 3, axis=-1)
        x = x + (ga * br_a + gb * br_b + gc * br_c) @ w_o[l]
        h = rms_norm(x, norm2_g[l])
        up = causal_dwconv(h @ ffn_up[l], ffn_conv_w[l])
        gt, vl = jnp.split(up, 2, axis=-1)
        x = x + (jax.nn.silu(gt) * vl) @ ffn_down[l]
    return rms_norm(x, final_g)
```

```python
import functools

import jax
import jax.numpy as jnp
from jax import lax
from jax.experimental import pallas as pl
from jax.experimental.pallas import tpu as pltpu

D_MODEL = 2048
BATCH = 4
SEQ = 2048
DEPTH = 2
ROWS = BATCH * SEQ

POOL_WIDTH = 1024
POOL_GROUPS = 4
POOL_WINDOWS = (2, 4, 8, 16)
POOL_GROUP_DIM = POOL_WIDTH // POOL_GROUPS
CONF_WIDTH = 1024
CONF_KERNEL = 31
SCONV_WIDTH = 1024
SCONV_KERNEL = 3
N_BRANCH = 3
OFF_POOL = POOL_WIDTH
OFF_CONF = OFF_POOL + 2 * CONF_WIDTH
OFF_SCONV = OFF_CONF + 3 * SCONV_WIDTH
D_IN = OFF_SCONV + N_BRANCH * D_MODEL
D_FF = 5504
FFN_KERNEL = 3
EPS = 1e-6

V7X_VMEM_BYTES = 64 * 1024 * 1024
SUBLANES = 8
LANES = 128

BF16 = jnp.bfloat16
F32 = jnp.float32


def _compiler_params(semantics, vmem_estimate_bytes):
    limit = min(int(vmem_estimate_bytes * 1.25) + (4 << 20), V7X_VMEM_BYTES - (6 << 20))
    return pltpu.CompilerParams(dimension_semantics=semantics, vmem_limit_bytes=limit)


def _sigmoid(x):
    return 1.0 / (1.0 + jnp.exp(-x))


def _rms_scale(x, g):
    ms = jnp.mean(x * x, axis=-1, keepdims=True)
    return x * lax.rsqrt(ms + EPS) * g


NORM_TM = 512


def _rmsnorm_kernel(x_ref, g_ref, o_ref):
    o_ref[...] = _rms_scale(x_ref[...], g_ref[...]).astype(o_ref.dtype)


def _rmsnorm_cast(x, g):
    tm = NORM_TM
    est = 2 * tm * D_MODEL * (4 + 2) + 4 * tm * D_MODEL * 4
    return pl.pallas_call(
        _rmsnorm_kernel,
        grid=(ROWS // tm,),
        in_specs=[pl.BlockSpec((tm, D_MODEL), lambda i: (i, 0)),
                  pl.BlockSpec((1, D_MODEL), lambda i: (0, 0))],
        out_specs=pl.BlockSpec((tm, D_MODEL), lambda i: (i, 0)),
        out_shape=jax.ShapeDtypeStruct((ROWS, D_MODEL), BF16),
        compiler_params=_compiler_params(("arbitrary",), est),
        name="rmsnorm_cast",
    )(x, g)


INPROJ_TM = 1024
INPROJ_TN = 1024
CAST_ROWS = 256


def _cast_weight_tile(w_ref, wbf_ref):
    rows = w_ref.shape[0]

    def body(r, carry):
        sl = pl.ds(pl.multiple_of(r * CAST_ROWS, CAST_ROWS), CAST_ROWS)
        wbf_ref[sl, :] = w_ref[sl, :].astype(BF16)
        return carry

    lax.fori_loop(0, rows // CAST_ROWS, body, 0)


def _inproj_kernel(h_ref, w_ref, o_ref, wbf_ref):
    @pl.when(pl.program_id(1) == 0)
    def _():
        _cast_weight_tile(w_ref, wbf_ref)

    o_ref[...] = jnp.dot(h_ref[...], wbf_ref[...], preferred_element_type=F32).astype(o_ref.dtype)


def _inproj(h, w_in, layer):
    tm, tn = INPROJ_TM, INPROJ_TN
    est = 2 * tm * D_MODEL * 2 + 2 * D_MODEL * tn * 4 + D_MODEL * tn * 2 + 2 * tm * tn * 2 + tm * tn * 4
    return pl.pallas_call(
        _inproj_kernel,
        grid=(D_IN // tn, ROWS // tm),
        in_specs=[pl.BlockSpec((tm, D_MODEL), lambda j, i: (i, 0)),
                  pl.BlockSpec((None, D_MODEL, tn), lambda j, i: (layer, 0, j))],
        out_specs=pl.BlockSpec((tm, tn), lambda j, i: (i, j)),
        out_shape=jax.ShapeDtypeStruct((ROWS, D_IN), BF16),
        scratch_shapes=[pltpu.VMEM((D_MODEL, tn), BF16)],
        compiler_params=_compiler_params(("arbitrary", "arbitrary"), est),
        name="inproj",
    )(h, w_in)


BR_TM = 256
BR_HALO = 32
BR_TILES_PER_SEQ = SEQ // BR_TM


def _branch_kernel(zc_ref, zp_ref, pw_ref, ps_ref, cw_ref, cb_ref, lg_ref, lb_ref, sw_ref,
                   a_ref, b_ref, c_ref, buf_ref, y_ref):
    tm, halo = BR_TM, BR_HALO
    t_idx = pl.program_id(0) % BR_TILES_PER_SEQ
    keep = jnp.where(t_idx == 0, 0.0, 1.0).astype(F32)

    def cur(lo, width):
        return zc_ref[:, lo:lo + width].astype(F32)

    def prev(lo, width):
        return zp_ref[:, lo:lo + width].astype(F32)

    buf_ref[0:halo, :] = prev(0, POOL_WIDTH) * keep
    buf_ref[halo:, :] = cur(0, POOL_WIDTH)
    pos = (t_idx * tm + 1 + lax.broadcasted_iota(jnp.int32, (tm, 1), 0)).astype(F32)
    for g, w in enumerate(POOL_WINDOWS):
        cols = slice(g * POOL_GROUP_DIM, (g + 1) * POOL_GROUP_DIM)
        x = buf_ref[halo:halo + tm, cols]
        ws = x
        for k in range(1, w):
            ws = ws + buf_ref[halo - k:halo - k + tm, cols]
        inv_count = 1.0 / jnp.minimum(pos, float(w))
        pooled = ws * inv_count - x
        mixed = jnp.dot(pooled.astype(BF16), pw_ref[g].astype(BF16), preferred_element_type=F32)
        a_ref[:, cols] = (mixed * ps_ref[:, cols]).astype(a_ref.dtype)

    v_off, g_off = OFF_POOL, OFF_POOL + CONF_WIDTH
    buf_ref[0:halo, :] = prev(v_off, CONF_WIDTH) * _sigmoid(prev(g_off, CONF_WIDTH)) * keep
    buf_ref[halo:, :] = cur(v_off, CONF_WIDTH) * _sigmoid(cur(g_off, CONF_WIDTH))
    base = halo - (CONF_KERNEL - 1)
    for c in range(CONF_WIDTH // LANES):
        cols = slice(c * LANES, (c + 1) * LANES)
        acc = cb_ref[:, cols] + cw_ref[0:1, cols] * buf_ref[base:base + tm, cols]
        for k in range(1, CONF_KERNEL):
            acc = acc + cw_ref[k:k + 1, cols] * buf_ref[base + k:base + k + tm, cols]
        y_ref[:, cols] = acc
    y = y_ref[...]
    mu = jnp.mean(y, axis=-1, keepdims=True)
    yc = y - mu
    var = jnp.mean(yc * yc, axis=-1, keepdims=True)
    yn = yc * lax.rsqrt(var + EPS) * lg_ref[...] + lb_ref[...]
    b_ref[...] = (yn * _sigmoid(yn)).astype(b_ref.dtype)

    b_off, c_off, h_off = OFF_CONF, OFF_CONF + SCONV_WIDTH, OFF_CONF + 2 * SCONV_WIDTH
    buf_ref[0:halo, :] = prev(c_off, SCONV_WIDTH) * prev(h_off, SCONV_WIDTH) * keep
    buf_ref[halo:, :] = cur(c_off, SCONV_WIDTH) * cur(h_off, SCONV_WIDTH)
    base = halo - (SCONV_KERNEL - 1)
    q = sw_ref[0:1, :] * buf_ref[base:base + tm, :]
    for k in range(1, SCONV_KERNEL):
        q = q + sw_ref[k:k + 1, :] * buf_ref[base + k:base + k + tm, :]
    c_ref[...] = (cur(b_off, SCONV_WIDTH) * q).astype(c_ref.dtype)


def _branches(z, pool_w, pool_scale, conf_w, conf_b, ln_g, ln_b, sconv_w):
    tm, halo = BR_TM, BR_HALO
    ratio = tm // halo
    row = lambda a: a.reshape(1, -1)
    full = lambda shape: pl.BlockSpec(shape, lambda i: (0,) * len(shape))
    out_spec = pl.BlockSpec((tm, POOL_WIDTH), lambda i: (i, 0))
    out_sds = jax.ShapeDtypeStruct((ROWS, POOL_WIDTH), BF16)
    est = (2 * (tm + halo) * OFF_SCONV * 2 + 2 * 3 * tm * 1024 * 2 + (2 * tm + halo) * 1024 * 4
           + 2 * POOL_GROUPS * POOL_GROUP_DIM * POOL_GROUP_DIM * 4 + 8 * tm * 1024 * 4)
    return pl.pallas_call(
        _branch_kernel,
        grid=(ROWS // tm,),
        in_specs=[pl.BlockSpec((tm, OFF_SCONV), lambda i: (i, 0)),
                  pl.BlockSpec((halo, OFF_SCONV), lambda i: (jnp.maximum(i * ratio - 1, 0), 0)),
                  full((POOL_GROUPS, POOL_GROUP_DIM, POOL_GROUP_DIM)),
                  full((1, POOL_WIDTH)),
                  full((CONF_KERNEL, CONF_WIDTH)),
                  full((1, CONF_WIDTH)), full((1, CONF_WIDTH)), full((1, CONF_WIDTH)),
                  full((SCONV_KERNEL, SCONV_WIDTH))],
        out_specs=[out_spec, out_spec, out_spec],
        out_shape=[out_sds, out_sds, out_sds],
        scratch_shapes=[pltpu.VMEM((halo + tm, 1024), F32), pltpu.VMEM((tm, 1024), F32)],
        compiler_params=_compiler_params(("arbitrary",), est),
        name="branches",
    )(z, z, pool_w, row(pool_scale), conf_w, row(conf_b), row(ln_g), row(ln_b), sconv_w)


MIX_TM = 256


def _mix_kernel(a_ref, b_ref, c_ref, zg_ref, gb_ref, x_ref, wa_ref, wb_ref, wc_ref, wo_ref, g_ref,
                xo_ref, ho_ref):
    def gate(k):
        cols = slice(k * D_MODEL, (k + 1) * D_MODEL)
        return _sigmoid(zg_ref[:, cols].astype(F32) + gb_ref[:, cols])

    def proj(act_ref, w_ref):
        return jnp.dot(act_ref[...], w_ref[...], preferred_element_type=F32)

    mix = gate(0) * proj(a_ref, wa_ref)
    mix = mix + gate(1) * proj(b_ref, wb_ref)
    mix = mix + gate(2) * proj(c_ref, wc_ref)
    xn = x_ref[...] + jnp.dot(mix.astype(BF16), wo_ref[...], preferred_element_type=F32)
    xo_ref[...] = xn
    ho_ref[...] = _rms_scale(xn, g_ref[...]).astype(ho_ref.dtype)


def _mix(a, b, c, z, gate_b, x, wa, wb, wc, wo, g):
    tm = MIX_TM
    gate_w = N_BRANCH * D_MODEL
    act_spec = pl.BlockSpec((tm, POOL_WIDTH), lambda i: (i, 0))
    row_spec = pl.BlockSpec((tm, D_MODEL), lambda i: (i, 0))
    const = lambda shape: pl.BlockSpec(shape, lambda i: (0, 0), pipeline_mode=pl.Buffered(1))
    est = (2 * 3 * tm * 1024 * 2 + 2 * tm * gate_w * 2 + 2 * tm * D_MODEL * (4 + 4 + 2)
           + (3 * 1024 + D_MODEL) * D_MODEL * 2 + 6 * tm * D_MODEL * 4)
    return pl.pallas_call(
        _mix_kernel,
        grid=(ROWS // tm,),
        in_specs=[act_spec, act_spec, act_spec,
                  pl.BlockSpec((tm, gate_w), lambda i: (i, OFF_SCONV // gate_w)),
                  pl.BlockSpec((1, gate_w), lambda i: (0, 0)),
                  row_spec,
                  const((POOL_WIDTH, D_MODEL)), const((CONF_WIDTH, D_MODEL)),
                  const((SCONV_WIDTH, D_MODEL)), const((D_MODEL, D_MODEL)),
                  pl.BlockSpec((1, D_MODEL), lambda i: (0, 0))],
        out_specs=[row_spec, row_spec],
        out_shape=[jax.ShapeDtypeStruct((ROWS, D_MODEL), F32),
                   jax.ShapeDtypeStruct((ROWS, D_MODEL), BF16)],
        compiler_params=_compiler_params(("arbitrary",), est),
        name="mix",
    )(a, b, c, z, gate_b.reshape(1, -1), x, wa, wb, wc, wo, g.reshape(1, -1))


UP_TM = 1024
UP_TN = 512
UP_TILES_PER_SEQ = SEQ // UP_TM
UP_COL_TILES = pl.cdiv(D_FF, UP_TN)
CARRY = SUBLANES


def _up_col_start(j, offset=0):
    tile = jnp.minimum(j * (UP_TN // LANES), (D_FF - UP_TN) // LANES)
    return (offset // LANES + tile) * LANES


def _ffn_up_kernel(h_ref, wg_ref, wv_ref, cg_ref, cv_ref, o_ref, wg_bf, wv_bf, ug_ref, uv_ref):
    tm = UP_TM
    i = pl.program_id(1)

    @pl.when(i == 0)
    def _():
        _cast_weight_tile(wg_ref.at[0], wg_bf)
        _cast_weight_tile(wv_ref.at[0], wv_bf)

    @pl.when(i % UP_TILES_PER_SEQ == 0)
    def _():
        ug_ref[0:CARRY, :] = jnp.zeros((CARRY, UP_TN), F32)
        uv_ref[0:CARRY, :] = jnp.zeros((CARRY, UP_TN), F32)

    h = h_ref[...]
    ug_ref[CARRY:, :] = jnp.dot(h, wg_bf[...], preferred_element_type=F32)
    uv_ref[CARRY:, :] = jnp.dot(h, wv_bf[...], preferred_element_type=F32)

    def conv(u_ref, cw_ref):
        base = CARRY - (FFN_KERNEL - 1)
        acc = cw_ref[0, 0:1, :] * u_ref[base:base + tm, :]
        for k in range(1, FFN_KERNEL):
            acc = acc + cw_ref[0, k:k + 1, :] * u_ref[base + k:base + k + tm, :]
        return acc

    gt = conv(ug_ref, cg_ref)
    vl = conv(uv_ref, cv_ref)
    o_ref[...] = (gt * _sigmoid(gt) * vl).astype(o_ref.dtype)

    ug_ref[0:CARRY, :] = ug_ref[tm:tm + CARRY, :]
    uv_ref[0:CARRY, :] = uv_ref[tm:tm + CARRY, :]


def _ffn_up(h, ffn_up, ffn_conv_w, layer):
    tm, tn = UP_TM, UP_TN
    w_spec = lambda off: pl.BlockSpec((pl.Element(1), pl.Element(D_MODEL), pl.Element(tn)),
                                      lambda j, i: (layer, 0, _up_col_start(j, off)))
    cw_spec = lambda off: pl.BlockSpec((pl.Element(1), pl.Element(FFN_KERNEL), pl.Element(tn)),
                                       lambda j, i: (layer, 0, _up_col_start(j, off)))
    est = (2 * tm * D_MODEL * 2 + 2 * 2 * D_MODEL * tn * 4 + 2 * D_MODEL * tn * 2 + 2 * tm * tn * 2
           + 2 * (tm + CARRY) * tn * 4 + 4 * tm * tn * 4)
    return pl.pallas_call(
        _ffn_up_kernel,
        grid=(UP_COL_TILES, ROWS // tm),
        in_specs=[pl.BlockSpec((pl.Element(tm), pl.Element(D_MODEL)), lambda j, i: (i * tm, 0)),
                  w_spec(0), w_spec(D_FF), cw_spec(0), cw_spec(D_FF)],
        out_specs=pl.BlockSpec((pl.Element(tm), pl.Element(tn)),
                               lambda j, i: (i * tm, _up_col_start(j))),
        out_shape=jax.ShapeDtypeStruct((ROWS, D_FF), BF16),
        scratch_shapes=[pltpu.VMEM((D_MODEL, tn), BF16), pltpu.VMEM((D_MODEL, tn), BF16),
                        pltpu.VMEM((CARRY + tm, tn), F32), pltpu.VMEM((CARRY + tm, tn), F32)],
        compiler_params=_compiler_params(("arbitrary", "arbitrary"), est),
        name="ffn_up",
    )(h, ffn_up, ffn_up, ffn_conv_w, ffn_conv_w)


DOWN_TM = 256


def _ffn_down_kernel(act_ref, w_ref, x_ref, g_ref, *out_refs, last):
    xn = x_ref[...] + jnp.dot(act_ref[...], w_ref[...], preferred_element_type=F32)
    normed = _rms_scale(xn, g_ref[...])
    if last:
        (y_ref,) = out_refs
        y_ref[...] = normed
    else:
        xo_ref, ho_ref = out_refs
        xo_ref[...] = xn
        ho_ref[...] = normed.astype(ho_ref.dtype)


def _ffn_down(act, wd, x, g, last):
    tm = DOWN_TM
    row_spec = pl.BlockSpec((tm, D_MODEL), lambda i: (i, 0))
    if last:
        out_specs = [row_spec]
        out_shape = [jax.ShapeDtypeStruct((ROWS, D_MODEL), F32)]
    else:
        out_specs = [row_spec, row_spec]
        out_shape = [jax.ShapeDtypeStruct((ROWS, D_MODEL), F32),
                     jax.ShapeDtypeStruct((ROWS, D_MODEL), BF16)]
    est = (2 * tm * D_FF * 2 + D_FF * D_MODEL * 2 + 2 * tm * D_MODEL * (4 + 4 + 2)
           + 3 * tm * D_MODEL * 4)
    return pl.pallas_call(
        functools.partial(_ffn_down_kernel, last=last),
        grid=(ROWS // tm,),
        in_specs=[pl.BlockSpec((tm, D_FF), lambda i: (i, 0)),
                  pl.BlockSpec((D_FF, D_MODEL), lambda i: (0, 0), pipeline_mode=pl.Buffered(1)),
                  row_spec,
                  pl.BlockSpec((1, D_MODEL), lambda i: (0, 0))],
        out_specs=out_specs,
        out_shape=out_shape,
        compiler_params=_compiler_params(("arbitrary",), est),
        name="ffn_down",
    )(act, wd, x, g.reshape(1, -1))


def kernel(x, norm1_g, w_in, gate_b, pool_w, pool_scale, pool_proj, conf_conv_w, conf_conv_b,
           conf_ln_g, conf_ln_b, conf_proj, sconv_w, sconv_proj, w_o, norm2_g, ffn_up, ffn_conv_w,
           ffn_down, final_g):
    assert x.shape == (BATCH, SEQ, D_MODEL) and w_in.shape == (DEPTH, D_MODEL, D_IN)
    xs = x.reshape(ROWS, D_MODEL)
    h = _rmsnorm_cast(xs, norm1_g[0].reshape(1, -1))
    for l in range(DEPTH):
        z = _inproj(h, w_in, l)
        a, b, c = _branches(z, pool_w[l], pool_scale[l], conf_conv_w[l], conf_conv_b[l],
                            conf_ln_g[l], conf_ln_b[l], sconv_w[l])
        xs, h = _mix(a, b, c, z, gate_b[l], xs,
                     pool_proj[l].astype(BF16), conf_proj[l].astype(BF16),
                     sconv_proj[l].astype(BF16), w_o[l].astype(BF16), norm2_g[l])
        act = _ffn_up(h, ffn_up, ffn_conv_w, l)
        last = l == DEPTH - 1
        g_next = final_g if last else norm1_g[l + 1]
        outs = _ffn_down(act, ffn_down[l].astype(BF16), xs, g_next, last)
        if last:
            (y,) = outs
        else:
            xs, h = outs
    return y.reshape(BATCH, SEQ, D_MODEL)
```

```python
import functools

import jax
import jax.numpy as jnp
from jax import lax
from jax.experimental import pallas as pl
from jax.experimental.pallas import tpu as pltpu

D_MODEL = 2048
BATCH = 4
SEQ = 2048
DEPTH = 2
ROWS = BATCH * SEQ

POOL_WIDTH = 1024
POOL_GROUPS = 4
POOL_WINDOWS = (2, 4, 8, 16)
POOL_GROUP_DIM = POOL_WIDTH // POOL_GROUPS
CONF_WIDTH = 1024
CONF_KERNEL = 31
SCONV_WIDTH = 1024
SCONV_KERNEL = 3
N_BRANCH = 3
BRANCH_WIDTH = 1024
OFF_POOL = POOL_WIDTH
OFF_CONF = OFF_POOL + 2 * CONF_WIDTH
OFF_SCONV = OFF_CONF + 3 * SCONV_WIDTH
GATE_WIDTH = N_BRANCH * D_MODEL
D_IN = OFF_SCONV + GATE_WIDTH
D_FF = 5504
FFN_KERNEL = 3
EPS = 1e-6

V7X_VMEM_BYTES = 64 * 1024 * 1024
SUBLANES = 8
LANES = 128

BF16 = jnp.bfloat16
F32 = jnp.float32


def _compiler_params(semantics, vmem_estimate_bytes):
    limit = min(int(vmem_estimate_bytes * 1.25) + (4 << 20), V7X_VMEM_BYTES - (6 << 20))
    return pltpu.CompilerParams(dimension_semantics=semantics, vmem_limit_bytes=limit)


def _sigmoid(x):
    return 1.0 / (1.0 + jnp.exp(-x))


def _rms_scale(x, g):
    ms = jnp.mean(x * x, axis=-1, keepdims=True)
    return x * lax.rsqrt(ms + EPS) * g


def _layer_row(stacked):
    return stacked.reshape(stacked.shape[0], 1, stacked.shape[1])


def _layer_spec(shape, layer, grid_rank, **kwargs):
    zeros = (0,) * len(shape)
    if grid_rank == 1:
        index_map = lambda i: (layer,) + zeros
    else:
        index_map = lambda j, i: (layer,) + zeros
    return pl.BlockSpec((None,) + tuple(shape), index_map, **kwargs)


NORM_TM = 512


def _rmsnorm_kernel(x_ref, g_ref, o_ref):
    o_ref[...] = _rms_scale(x_ref[...], g_ref[...]).astype(o_ref.dtype)


def _rmsnorm_cast(x, g, layer):
    tm = NORM_TM
    est = 2 * tm * D_MODEL * (4 + 2) + 4 * tm * D_MODEL * 4
    return pl.pallas_call(
        _rmsnorm_kernel,
        grid=(ROWS // tm,),
        in_specs=[pl.BlockSpec((tm, D_MODEL), lambda i: (i, 0)),
                  _layer_spec((1, D_MODEL), layer, 1)],
        out_specs=pl.BlockSpec((tm, D_MODEL), lambda i: (i, 0)),
        out_shape=jax.ShapeDtypeStruct((ROWS, D_MODEL), BF16),
        compiler_params=_compiler_params(("arbitrary",), est),
        name="rmsnorm_cast",
    )(x, _layer_row(g))


INPROJ_TM = 1024
INPROJ_TN = 1024
CAST_ROWS = 256


def _cast_weight_tile(w_ref, wbf_ref, lane_shift=0):
    rows, cols = wbf_ref.shape
    keep = cols - lane_shift

    def body(r, carry):
        sl = pl.ds(pl.multiple_of(r * CAST_ROWS, CAST_ROWS), CAST_ROWS)
        wbf_ref[sl, 0:keep] = w_ref[sl, lane_shift:cols].astype(BF16)
        if lane_shift:
            wbf_ref[sl, keep:cols] = jnp.zeros((CAST_ROWS, lane_shift), BF16)
        return carry

    lax.fori_loop(0, rows // CAST_ROWS, body, 0)


def _inproj_kernel(h_ref, w_ref, o_ref, wbf_ref):
    @pl.when(pl.program_id(1) == 0)
    def _():
        _cast_weight_tile(w_ref, wbf_ref)

    o_ref[...] = jnp.dot(h_ref[...], wbf_ref[...], preferred_element_type=F32).astype(o_ref.dtype)


def _inproj(h, w_in, layer):
    tm, tn = INPROJ_TM, INPROJ_TN
    est = 2 * tm * D_MODEL * 2 + 2 * D_MODEL * tn * 4 + D_MODEL * tn * 2 + 2 * tm * tn * 2 + tm * tn * 4
    return pl.pallas_call(
        _inproj_kernel,
        grid=(D_IN // tn, ROWS // tm),
        in_specs=[pl.BlockSpec((tm, D_MODEL), lambda j, i: (i, 0)),
                  pl.BlockSpec((None, D_MODEL, tn), lambda j, i: (layer, 0, j))],
        out_specs=pl.BlockSpec((tm, tn), lambda j, i: (i, j)),
        out_shape=jax.ShapeDtypeStruct((ROWS, D_IN), BF16),
        scratch_shapes=[pltpu.VMEM((D_MODEL, tn), BF16)],
        compiler_params=_compiler_params(("arbitrary", "arbitrary"), est),
        name="inproj",
    )(h, w_in)


TM_TM = 256
TM_HALO = 32
TM_TILES = ROWS // TM_TM
TM_TILES_PER_SEQ = SEQ // TM_TM
TM_SHIFT_ROWS = TM_HALO + TM_TM - SUBLANES
MIX_NC = 512


def _shifted_causal_conv(buf_ref, shf_ref, cw_ref, bias, cols, taps):
    base = TM_HALO - (taps - 1)
    residues = sorted({(base + k) % SUBLANES for k in range(taps)} - {0})
    for r in residues:
        shf_ref[r - 1, :, :] = buf_ref[r:r + TM_SHIFT_ROWS, cols]
    acc = bias
    for k in range(taps):
        a, r = divmod(base + k, SUBLANES)
        lo = a * SUBLANES
        src = buf_ref[lo:lo + TM_TM, cols] if r == 0 else shf_ref[r - 1, lo:lo + TM_TM, :]
        term = cw_ref[k:k + 1, cols] * src
        acc = term if acc is None else acc + term
    return acc


def _tokenmix_step(s, t_idx, zc_ref, zp_ref, zg_ref, x_ref, pw_ref, ps_ref, cw_ref, cb_ref, lg_ref,
                   lb_ref, sw_ref, gb_ref, wa_ref, wb_ref, wc_ref, wo_ref, g_ref, xo_ref, ho_ref,
                   write_feats, read_feats, buf_ref, y_ref, shf_ref, mix_ref):
    tm, halo = TM_TM, TM_HALO
    fa_ref, fb_ref, fc_ref = write_feats
    keep = jnp.where(t_idx == 0, 0.0, 1.0).astype(F32)

    def cur(lo, width=BRANCH_WIDTH):
        return zc_ref[:, lo:lo + width].astype(F32)

    def prev(lo, width=BRANCH_WIDTH):
        return zp_ref[:, lo:lo + width].astype(F32)

    for c in range(D_MODEL // MIX_NC):
        cols = slice(c * MIX_NC, (c + 1) * MIX_NC)
        acc = None
        for k, (f_ref, w_ref) in enumerate(zip(read_feats, (wa_ref, wb_ref, wc_ref))):
            gcols = slice(k * D_MODEL + c * MIX_NC, k * D_MODEL + (c + 1) * MIX_NC)
            gate = _sigmoid(zg_ref[:, gcols].astype(F32) + gb_ref[:, gcols])
            term = gate * jnp.dot(f_ref[...], w_ref[:, cols], preferred_element_type=F32)
            acc = term if acc is None else acc + term
        mix_ref[:, cols] = acc.astype(mix_ref.dtype)

    buf_ref[0:halo, :] = prev(0) * keep
    buf_ref[halo:, :] = cur(0)
    pos = (t_idx * tm + 1 + lax.broadcasted_iota(jnp.int32, (tm, 1), 0)).astype(F32)
    for g, w in enumerate(POOL_WINDOWS):
        cols = slice(g * POOL_GROUP_DIM, (g + 1) * POOL_GROUP_DIM)
        x = buf_ref[halo:halo + tm, cols]
        ws = x
        for k in range(1, w):
            ws = ws + buf_ref[halo - k:halo - k + tm, cols]
        inv_count = 1.0 / jnp.minimum(pos, float(w))
        pooled = ws * inv_count - x
        mixed = jnp.dot(pooled.astype(BF16), pw_ref[g].astype(BF16), preferred_element_type=F32)
        fa_ref[:, cols] = (mixed * ps_ref[:, cols]).astype(fa_ref.dtype)

    b_off, c_off, h_off = OFF_CONF, OFF_CONF + SCONV_WIDTH, OFF_CONF + 2 * SCONV_WIDTH
    buf_ref[0:halo, :] = prev(c_off) * prev(h_off) * keep
    buf_ref[halo:, :] = cur(c_off) * cur(h_off)
    base = halo - (SCONV_KERNEL - 1)
    q = sw_ref[0:1, :] * buf_ref[base:base + tm, :]
    for k in range(1, SCONV_KERNEL):
        q = q + sw_ref[k:k + 1, :] * buf_ref[base + k:base + k + tm, :]
    fc_ref[...] = (cur(b_off) * q).astype(fc_ref.dtype)

    v_off, g_off = OFF_POOL, OFF_POOL + CONF_WIDTH
    buf_ref[0:halo, :] = prev(v_off) * _sigmoid(prev(g_off)) * keep
    buf_ref[halo:, :] = cur(v_off) * _sigmoid(cur(g_off))

    n_chunks = CONF_WIDTH // LANES
    out_nc = D_MODEL // n_chunks
    for c in range(n_chunks):
        ocols = slice(c * out_nc, (c + 1) * out_nc)
        proj = jnp.dot(mix_ref[...], wo_ref[:, ocols], preferred_element_type=F32)
        cols = slice(c * LANES, (c + 1) * LANES)
        y_ref[:, cols] = _shifted_causal_conv(buf_ref, shf_ref, cw_ref, cb_ref[:, cols], cols,
                                              CONF_KERNEL)
        xo_ref[:, ocols] = x_ref[:, ocols] + proj

    y = y_ref[...]
    mu = jnp.mean(y, axis=-1, keepdims=True)
    yc = y - mu
    var = jnp.mean(yc * yc, axis=-1, keepdims=True)
    yn = yc * lax.rsqrt(var + EPS) * lg_ref[...] + lb_ref[...]
    fb_ref[...] = (yn * _sigmoid(yn)).astype(fb_ref.dtype)
    ho_ref[...] = _rms_scale(xo_ref[...], g_ref[...]).astype(ho_ref.dtype)


def _tokenmix_kernel(zc_ref, zp_ref, zg_ref, x_ref, pw_ref, ps_ref, cw_ref, cb_ref, lg_ref, lb_ref,
                     sw_ref, gb_ref, wa_ref, wb_ref, wc_ref, wo_ref, g_ref, xo_ref, ho_ref,
                     a0_ref, b0_ref, c0_ref, a1_ref, b1_ref, c1_ref, buf_ref, y_ref, shf_ref, mix_ref):
    s = pl.program_id(0)
    t_idx = jnp.minimum(s, TM_TILES - 1) % TM_TILES_PER_SEQ
    slots = ((a0_ref, b0_ref, c0_ref), (a1_ref, b1_ref, c1_ref))

    @pl.when(s == 0)
    def _():
        for ref in slots[1]:
            ref[...] = jnp.zeros(ref.shape, ref.dtype)

    def step(write_slot, read_slot):
        _tokenmix_step(s, t_idx, zc_ref, zp_ref, zg_ref, x_ref, pw_ref, ps_ref, cw_ref, cb_ref,
                       lg_ref, lb_ref, sw_ref, gb_ref, wa_ref, wb_ref, wc_ref, wo_ref, g_ref,
                       xo_ref, ho_ref, slots[write_slot], slots[read_slot], buf_ref, y_ref,
                       shf_ref, mix_ref)

    @pl.when(s % 2 == 0)
    def _():
        step(0, 1)

    @pl.when(s % 2 == 1)
    def _():
        step(1, 0)


def _tokenmix(z, x, layer, pool_w, pool_scale, conf_w, conf_b, ln_g, ln_b, sconv_w, gate_b,
              wa, wb, wc, wo, norm_g):
    tm, halo = TM_TM, TM_HALO
    ratio = tm // halo
    feat_tile = lambda s: jnp.minimum(s, TM_TILES - 1)
    mix_tile = lambda s: jnp.maximum(s - 1, 0)
    lspec = lambda shape, **kw: _layer_spec(shape, layer, 1, **kw)
    once = dict(pipeline_mode=pl.Buffered(1))
    row_spec = pl.BlockSpec((tm, D_MODEL), lambda s: (mix_tile(s), 0))
    feat = pltpu.VMEM((tm, BRANCH_WIDTH), BF16)
    est = (2 * (2 * tm + halo) * OFF_SCONV * 2 + 2 * tm * D_MODEL * (4 + 4 + 2)
           + POOL_WIDTH * POOL_GROUP_DIM * 4 + (3 * BRANCH_WIDTH + D_MODEL) * D_MODEL * 2
           + (2 * tm + halo) * BRANCH_WIDTH * 4 + (SUBLANES - 1) * TM_SHIFT_ROWS * LANES * 4
           + 6 * tm * BRANCH_WIDTH * 2 + tm * D_MODEL * 2 + 6 * tm * MIX_NC * 4)
    return pl.pallas_call(
        _tokenmix_kernel,
        grid=(TM_TILES + 1,),
        in_specs=[pl.BlockSpec((tm, OFF_SCONV), lambda s: (feat_tile(s), 0)),
                  pl.BlockSpec((halo, OFF_SCONV),
                               lambda s: (jnp.maximum(feat_tile(s) * ratio - 1, 0), 0)),
                  pl.BlockSpec((tm, GATE_WIDTH), lambda s: (mix_tile(s), OFF_SCONV // GATE_WIDTH)),
                  row_spec,
                  lspec((POOL_GROUPS, POOL_GROUP_DIM, POOL_GROUP_DIM), **once),
                  lspec((1, POOL_WIDTH)),
                  lspec((CONF_KERNEL, CONF_WIDTH)),
                  lspec((1, CONF_WIDTH)), lspec((1, CONF_WIDTH)), lspec((1, CONF_WIDTH)),
                  lspec((SCONV_KERNEL, SCONV_WIDTH)),
                  lspec((1, GATE_WIDTH)),
                  lspec((BRANCH_WIDTH, D_MODEL), **once), lspec((BRANCH_WIDTH, D_MODEL), **once),
                  lspec((BRANCH_WIDTH, D_MODEL), **once), lspec((D_MODEL, D_MODEL), **once),
                  lspec((1, D_MODEL))],
        out_specs=[row_spec, row_spec],
        out_shape=[jax.ShapeDtypeStruct((ROWS, D_MODEL), F32),
                   jax.ShapeDtypeStruct((ROWS, D_MODEL), BF16)],
        scratch_shapes=[feat, feat, feat, feat, feat, feat,
                        pltpu.VMEM((halo + tm, BRANCH_WIDTH), F32),
                        pltpu.VMEM((tm, BRANCH_WIDTH), F32),
                        pltpu.VMEM((SUBLANES - 1, TM_SHIFT_ROWS, LANES), F32),
                        pltpu.VMEM((tm, D_MODEL), BF16)],
        compiler_params=_compiler_params(("arbitrary",), est),
        name="tokenmix",
    )(z, z, z, x, pool_w, _layer_row(pool_scale), conf_w, _layer_row(conf_b), _layer_row(ln_g),
      _layer_row(ln_b), sconv_w, _layer_row(gate_b), wa, wb, wc, wo, _layer_row(norm_g))


UP_TM = 1024
UP_TN = 512
UP_TILES_PER_SEQ = SEQ // UP_TM
UP_COL_TILES = pl.cdiv(D_FF, UP_TN)
D_FF_PAD = UP_COL_TILES * UP_TN
UP_LAST_SHIFT = D_FF_PAD - D_FF
CARRY = SUBLANES


def _up_col_start(j, offset=0):
    tile = jnp.minimum(j * (UP_TN // LANES), (D_FF - UP_TN) // LANES)
    return (offset // LANES + tile) * LANES


def _ffn_up_kernel(h_ref, wg_ref, wv_ref, cg_ref, cv_ref, wd_ref, o_ref, wdo_ref,
                   wg_bf, wv_bf, cw_ref, ug_ref, uv_ref):
    tm, tn = UP_TM, UP_TN
    j, i = pl.program_id(0), pl.program_id(1)
    last = UP_COL_TILES - 1

    def stage_weights(shift):
        keep = tn - shift
        _cast_weight_tile(wg_ref.at[0], wg_bf, shift)
        _cast_weight_tile(wv_ref.at[0], wv_bf, shift)
        for r, ref in enumerate((cg_ref, cv_ref)):
            rows = slice(r * SUBLANES, r * SUBLANES + FFN_KERNEL)
            cw_ref[rows, 0:keep] = ref[0, :, shift:tn]
            if shift:
                cw_ref[rows, keep:tn] = jnp.zeros((FFN_KERNEL, shift), F32)
        wdo_ref[0:keep, :] = wd_ref[0:keep, :].astype(BF16)
        if shift:
            wdo_ref[keep:tn, :] = jnp.zeros((shift, D_MODEL), BF16)

    @pl.when((i == 0) & (j != last))
    def _():
        stage_weights(0)

    @pl.when((i == 0) & (j == last))
    def _():
        stage_weights(UP_LAST_SHIFT)

    @pl.when(i % UP_TILES_PER_SEQ == 0)
    def _():
        ug_ref[0:CARRY, :] = jnp.zeros((CARRY, tn), F32)
        uv_ref[0:CARRY, :] = jnp.zeros((CARRY, tn), F32)

    def conv(u_ref, w_row):
        base = CARRY - (FFN_KERNEL - 1)
        acc = None
        for k in range(FFN_KERNEL):
            term = cw_ref[w_row + k:w_row + k + 1, :] * u_ref[base + k:base + k + tm, :]
            acc = term if acc is None else acc + term
        return acc

    h = h_ref[...]
    ug_ref[CARRY:, :] = jnp.dot(h, wg_bf[...], preferred_element_type=F32)
    uv_ref[CARRY:, :] = jnp.dot(h, wv_bf[...], preferred_element_type=F32)
    gt = conv(ug_ref, 0)
    vl = conv(uv_ref, SUBLANES)
    o_ref[...] = (gt * _sigmoid(gt) * vl).astype(o_ref.dtype)

    ug_ref[0:CARRY, :] = ug_ref[tm:tm + CARRY, :]
    uv_ref[0:CARRY, :] = uv_ref[tm:tm + CARRY, :]


def _ffn_up(h, ffn_up, ffn_conv_w, ffn_down, layer):
    tm, tn = UP_TM, UP_TN
    w_spec = lambda off: pl.BlockSpec((pl.Element(1), pl.Element(D_MODEL), pl.Element(tn)),
                                      lambda j, i: (layer, 0, _up_col_start(j, off)))
    cw_spec = lambda off: pl.BlockSpec((pl.Element(1), pl.Element(FFN_KERNEL), pl.Element(tn)),
                                       lambda j, i: (layer, 0, _up_col_start(j, off)))
    est = (2 * tm * D_MODEL * 2 + 2 * 2 * D_MODEL * tn * 4 + 2 * D_MODEL * tn * 2 + 2 * tm * tn * 2
           + 2 * (tm + CARRY) * tn * 4 + 2 * tn * D_MODEL * (4 + 2) + 4 * tm * tn * 4)
    return pl.pallas_call(
        _ffn_up_kernel,
        grid=(UP_COL_TILES, ROWS // tm),
        in_specs=[pl.BlockSpec((tm, D_MODEL), lambda j, i: (i, 0)),
                  w_spec(0), w_spec(D_FF), cw_spec(0), cw_spec(D_FF),
                  pl.BlockSpec((None, tn, D_MODEL), lambda j, i: (layer, j, 0))],
        out_specs=[pl.BlockSpec((tm, tn), lambda j, i: (i, j)),
                   pl.BlockSpec((tn, D_MODEL), lambda j, i: (j, 0))],
        out_shape=[jax.ShapeDtypeStruct((ROWS, D_FF_PAD), BF16),
                   jax.ShapeDtypeStruct((D_FF_PAD, D_MODEL), BF16)],
        scratch_shapes=[pltpu.VMEM((D_MODEL, tn), BF16), pltpu.VMEM((D_MODEL, tn), BF16),
                        pltpu.VMEM((2 * SUBLANES, tn), F32),
                        pltpu.VMEM((CARRY + tm, tn), F32), pltpu.VMEM((CARRY + tm, tn), F32)],
        compiler_params=_compiler_params(("arbitrary", "arbitrary"), est),
        name="ffn_up",
    )(h, ffn_up, ffn_up, ffn_conv_w, ffn_conv_w, ffn_down)


DOWN_TM = 256


def _ffn_down_kernel(act_ref, w_ref, x_ref, g_ref, *out_refs, last):
    xn = x_ref[...] + jnp.dot(act_ref[...], w_ref[...], preferred_element_type=F32)
    normed = _rms_scale(xn, g_ref[...])
    if last:
        (y_ref,) = out_refs
        y_ref[...] = normed
    else:
        xo_ref, ho_ref = out_refs
        xo_ref[...] = xn
        ho_ref[...] = normed.astype(ho_ref.dtype)


def _ffn_down(act, wd, x, g, last):
    tm = DOWN_TM
    row_spec = pl.BlockSpec((tm, D_MODEL), lambda i: (i, 0))
    if last:
        out_specs = [row_spec]
        out_shape = [jax.ShapeDtypeStruct((ROWS, D_MODEL), F32)]
    else:
        out_specs = [row_spec, row_spec]
        out_shape = [jax.ShapeDtypeStruct((ROWS, D_MODEL), F32),
                     jax.ShapeDtypeStruct((ROWS, D_MODEL), BF16)]
    est = (2 * tm * D_FF_PAD * 2 + D_FF_PAD * D_MODEL * 2 + 2 * tm * D_MODEL * (4 + 4 + 2)
           + 3 * tm * D_MODEL * 4)
    return pl.pallas_call(
        functools.partial(_ffn_down_kernel, last=last),
        grid=(ROWS // tm,),
        in_specs=[pl.BlockSpec((tm, D_FF_PAD), lambda i: (i, 0)),
                  pl.BlockSpec((D_FF_PAD, D_MODEL), lambda i: (0, 0), pipeline_mode=pl.Buffered(1)),
                  row_spec,
                  pl.BlockSpec((1, D_MODEL), lambda i: (0, 0))],
        out_specs=out_specs,
        out_shape=out_shape,
        compiler_params=_compiler_params(("arbitrary",), est),
        name="ffn_down",
    )(act, wd, x, g.reshape(1, -1))


def kernel(x, norm1_g, w_in, gate_b, pool_w, pool_scale, pool_proj, conf_conv_w, conf_conv_b,
           conf_ln_g, conf_ln_b, conf_proj, sconv_w, sconv_proj, w_o, norm2_g, ffn_up, ffn_conv_w,
           ffn_down, final_g):
    assert x.shape == (BATCH, SEQ, D_MODEL) and w_in.shape == (DEPTH, D_MODEL, D_IN)
    xs = x.reshape(ROWS, D_MODEL)
    wa, wb, wc, wo = (w.astype(BF16) for w in (pool_proj, conf_proj, sconv_proj, w_o))
    h = _rmsnorm_cast(xs, norm1_g, 0)
    for l in range(DEPTH):
        z = _inproj(h, w_in, l)
        xs, h = _tokenmix(z, xs, l, pool_w, pool_scale, conf_conv_w, conf_conv_b, conf_ln_g,
                          conf_ln_b, sconv_w, gate_b, wa, wb, wc, wo, norm2_g)
        act, wd = _ffn_up(h, ffn_up, ffn_conv_w, ffn_down, l)
        last = l == DEPTH - 1
        g_next = final_g if last else norm1_g[l + 1]
        outs = _ffn_down(act, wd, xs, g_next, last)
        if last:
            (y,) = outs
        else:
            xs, h = outs
    return y.reshape(BATCH, SEQ, D_MODEL)
```

```python
import functools

import jax
import jax.numpy as jnp
from jax import lax
from jax.experimental import pallas as pl
from jax.experimental.pallas import tpu as pltpu

D_MODEL = 2048
BATCH = 4
SEQ = 2048
DEPTH = 2
ROWS = BATCH * SEQ

POOL_WIDTH = 1024
POOL_GROUPS = 4
POOL_WINDOWS = (2, 4, 8, 16)
POOL_GROUP_DIM = POOL_WIDTH // POOL_GROUPS
CONF_WIDTH = 1024
CONF_KERNEL = 31
SCONV_WIDTH = 1024
SCONV_KERNEL = 3
N_BRANCH = 3
BRANCH_WIDTH = 1024
OFF_POOL = POOL_WIDTH
OFF_CONF = OFF_POOL + 2 * CONF_WIDTH
OFF_SCONV = OFF_CONF + 3 * SCONV_WIDTH
GATE_WIDTH = N_BRANCH * D_MODEL
D_IN = OFF_SCONV + GATE_WIDTH
D_FF = 5504
FFN_KERNEL = 3
EPS = 1e-6

V7X_VMEM_BYTES = 64 * 1024 * 1024
SUBLANES = 8
LANES = 128

BF16 = jnp.bfloat16
F32 = jnp.float32


def _compiler_params(semantics, vmem_estimate_bytes):
    limit = min(int(vmem_estimate_bytes * 1.25) + (4 << 20), V7X_VMEM_BYTES - (4 << 20))
    return pltpu.CompilerParams(dimension_semantics=semantics, vmem_limit_bytes=limit)


def _sigmoid(x):
    return 1.0 / (1.0 + jnp.exp(-x))


def _rms_scale(x, g):
    ms = jnp.mean(x * x, axis=-1, keepdims=True)
    return x * lax.rsqrt(ms + EPS) * g


def _layer_row(stacked):
    return stacked.reshape(stacked.shape[0], 1, stacked.shape[1])


def _layer_spec(shape, layer, grid_rank, **kwargs):
    zeros = (0,) * len(shape)
    if grid_rank == 1:
        index_map = lambda i: (layer,) + zeros
    else:
        index_map = lambda j, i: (layer,) + zeros
    return pl.BlockSpec((None,) + tuple(shape), index_map, **kwargs)


NORM_TM = 512


def _rmsnorm_kernel(x_ref, g_ref, o_ref):
    o_ref[...] = _rms_scale(x_ref[...], g_ref[...]).astype(o_ref.dtype)


def _rmsnorm_cast(x, g, layer):
    tm = NORM_TM
    est = 2 * tm * D_MODEL * (4 + 2) + 4 * tm * D_MODEL * 4
    return pl.pallas_call(
        _rmsnorm_kernel,
        grid=(ROWS // tm,),
        in_specs=[pl.BlockSpec((tm, D_MODEL), lambda i: (i, 0)),
                  _layer_spec((1, D_MODEL), layer, 1)],
        out_specs=pl.BlockSpec((tm, D_MODEL), lambda i: (i, 0)),
        out_shape=jax.ShapeDtypeStruct((ROWS, D_MODEL), BF16),
        compiler_params=_compiler_params(("arbitrary",), est),
        name="rmsnorm_cast",
    )(x, _layer_row(g))


INPROJ_TM = 1024
INPROJ_TN = 1024
CAST_ROWS = 256


def _cast_weight_tile(w_ref, wbf_ref, lane_shift=0):
    rows, cols = wbf_ref.shape
    keep = cols - lane_shift

    def body(r, carry):
        sl = pl.ds(pl.multiple_of(r * CAST_ROWS, CAST_ROWS), CAST_ROWS)
        wbf_ref[sl, 0:keep] = w_ref[sl, lane_shift:cols].astype(BF16)
        if lane_shift:
            wbf_ref[sl, keep:cols] = jnp.zeros((CAST_ROWS, lane_shift), BF16)
        return carry

    lax.fori_loop(0, rows // CAST_ROWS, body, 0)


def _inproj_kernel(h_ref, w_ref, o_ref, wbf_ref):
    @pl.when(pl.program_id(1) == 0)
    def _():
        _cast_weight_tile(w_ref, wbf_ref)

    o_ref[...] = jnp.dot(h_ref[...], wbf_ref[...], preferred_element_type=F32).astype(o_ref.dtype)


def _inproj(h, w_in, layer):
    tm, tn = INPROJ_TM, INPROJ_TN
    est = 2 * tm * D_MODEL * 2 + 2 * D_MODEL * tn * 4 + D_MODEL * tn * 2 + 2 * tm * tn * 2 + tm * tn * 4
    return pl.pallas_call(
        _inproj_kernel,
        grid=(D_IN // tn, ROWS // tm),
        in_specs=[pl.BlockSpec((tm, D_MODEL), lambda j, i: (i, 0)),
                  pl.BlockSpec((None, D_MODEL, tn), lambda j, i: (layer, 0, j))],
        out_specs=pl.BlockSpec((tm, tn), lambda j, i: (i, j)),
        out_shape=jax.ShapeDtypeStruct((ROWS, D_IN), BF16),
        scratch_shapes=[pltpu.VMEM((D_MODEL, tn), BF16)],
        compiler_params=_compiler_params(("arbitrary", "arbitrary"), est),
        name="inproj",
    )(h, w_in)


TM_TM = 256
TM_HALO = 32
TM_TILES = ROWS // TM_TM
TM_TILES_PER_SEQ = SEQ // TM_TM
TM_SHIFT_ROWS = TM_HALO + TM_TM - SUBLANES
MIX_NC = 512


def _shifted_causal_conv(buf_ref, shf_ref, cw_ref, bias, cols, taps):
    base = TM_HALO - (taps - 1)
    residues = sorted({(base + k) % SUBLANES for k in range(taps)} - {0})
    for r in residues:
        shf_ref[r - 1, :, :] = buf_ref[r:r + TM_SHIFT_ROWS, cols]
    acc = bias
    for k in range(taps):
        a, r = divmod(base + k, SUBLANES)
        lo = a * SUBLANES
        src = buf_ref[lo:lo + TM_TM, cols] if r == 0 else shf_ref[r - 1, lo:lo + TM_TM, :]
        term = cw_ref[k:k + 1, cols] * src
        acc = term if acc is None else acc + term
    return acc


def _tokenmix_step(t_idx, zc_ref, zp_ref, zg_ref, x_ref, pw_ref, ps_ref, cw_ref, cb_ref, lg_ref,
                   lb_ref, sw_ref, gb_ref, wa_ref, wb_ref, wc_ref, wo_ref, g_ref, xo_ref, ho_ref,
                   write_feats, read_feats, buf_ref, y_ref, shf_ref, mix_ref, pool_ref):
    tm, halo = TM_TM, TM_HALO
    fa_ref, fb_ref, fc_ref = write_feats
    keep = jnp.where(t_idx == 0, 0.0, 1.0).astype(F32)

    def cur(lo, width=BRANCH_WIDTH):
        return zc_ref[:, lo:lo + width].astype(F32)

    def prev(lo, width=BRANCH_WIDTH):
        return zp_ref[:, lo:lo + width].astype(F32)

    for c in range(D_MODEL // MIX_NC):
        cols = slice(c * MIX_NC, (c + 1) * MIX_NC)
        acc = None
        for k, (f_ref, w_ref) in enumerate(zip(read_feats, (wa_ref, wb_ref, wc_ref))):
            gcols = slice(k * D_MODEL + c * MIX_NC, k * D_MODEL + (c + 1) * MIX_NC)
            gate = _sigmoid(zg_ref[:, gcols].astype(F32) + gb_ref[:, gcols])
            term = gate * jnp.dot(f_ref[...], w_ref[:, cols], preferred_element_type=F32)
            acc = term if acc is None else acc + term
        mix_ref[:, cols] = acc.astype(mix_ref.dtype)

    buf_ref[0:halo, :] = prev(0) * keep
    buf_ref[halo:, :] = cur(0)
    pos = (t_idx * tm + 1 + lax.broadcasted_iota(jnp.int32, (tm, 1), 0)).astype(F32)
    ext = halo + tm
    pool_ref[0:SUBLANES, :] = jnp.zeros((SUBLANES, POOL_GROUP_DIM), F32)
    for g, w in enumerate(POOL_WINDOWS):
        cols = slice(g * POOL_GROUP_DIM, (g + 1) * POOL_GROUP_DIM)
        ws = buf_ref[:, cols]
        x = ws[halo:, :]
        d = 1
        while d < w:
            pool_ref[d:d + ext, :] = ws
            ws = ws + pool_ref[0:ext, :]
            d *= 2
        inv_count = 1.0 / jnp.minimum(pos, float(w))
        pooled = ws[halo:, :] * inv_count - x
        mixed = jnp.dot(pooled.astype(BF16), pw_ref[g].astype(BF16), preferred_element_type=F32)
        fa_ref[:, cols] = (mixed * ps_ref[:, cols]).astype(fa_ref.dtype)

    b_off, c_off, h_off = OFF_CONF, OFF_CONF + SCONV_WIDTH, OFF_CONF + 2 * SCONV_WIDTH
    buf_ref[0:halo, :] = prev(c_off) * prev(h_off) * keep
    buf_ref[halo:, :] = cur(c_off) * cur(h_off)
    base = halo - (SCONV_KERNEL - 1)
    q = sw_ref[0:1, :] * buf_ref[base:base + tm, :]
    for k in range(1, SCONV_KERNEL):
        q = q + sw_ref[k:k + 1, :] * buf_ref[base + k:base + k + tm, :]
    fc_ref[...] = (cur(b_off) * q).astype(fc_ref.dtype)

    v_off, g_off = OFF_POOL, OFF_POOL + CONF_WIDTH
    buf_ref[0:halo, :] = prev(v_off) * _sigmoid(prev(g_off)) * keep
    buf_ref[halo:, :] = cur(v_off) * _sigmoid(cur(g_off))

    n_chunks = CONF_WIDTH // LANES
    out_nc = D_MODEL // n_chunks
    for c in range(n_chunks):
        ocols = slice(c * out_nc, (c + 1) * out_nc)
        proj = jnp.dot(mix_ref[...], wo_ref[:, ocols], preferred_element_type=F32)
        cols = slice(c * LANES, (c + 1) * LANES)
        y_ref[:, cols] = _shifted_causal_conv(buf_ref, shf_ref, cw_ref, cb_ref[:, cols], cols,
                                              CONF_KERNEL)
        xo_ref[:, ocols] = x_ref[:, ocols] + proj

    y = y_ref[...]
    mu = jnp.mean(y, axis=-1, keepdims=True)
    yc = y - mu
    var = jnp.mean(yc * yc, axis=-1, keepdims=True)
    yn = yc * lax.rsqrt(var + EPS) * lg_ref[...] + lb_ref[...]
    fb_ref[...] = (yn * _sigmoid(yn)).astype(fb_ref.dtype)
    ho_ref[...] = _rms_scale(xo_ref[...], g_ref[...]).astype(ho_ref.dtype)


def _tokenmix_kernel(zc_ref, zp_ref, zg_ref, x_ref, pw_ref, ps_ref, cw_ref, cb_ref, lg_ref, lb_ref,
                     sw_ref, gb_ref, wa_ref, wb_ref, wc_ref, wo_ref, g_ref, xo_ref, ho_ref,
                     a0_ref, b0_ref, c0_ref, a1_ref, b1_ref, c1_ref, buf_ref, y_ref, shf_ref, mix_ref,
                     pool_ref):
    s = pl.program_id(0)
    t_idx = jnp.minimum(s, TM_TILES - 1) % TM_TILES_PER_SEQ
    slots = ((a0_ref, b0_ref, c0_ref), (a1_ref, b1_ref, c1_ref))

    @pl.when(s == 0)
    def _():
        for ref in slots[1]:
            ref[...] = jnp.zeros(ref.shape, ref.dtype)

    def step(write_slot, read_slot):
        _tokenmix_step(t_idx, zc_ref, zp_ref, zg_ref, x_ref, pw_ref, ps_ref, cw_ref, cb_ref,
                       lg_ref, lb_ref, sw_ref, gb_ref, wa_ref, wb_ref, wc_ref, wo_ref, g_ref,
                       xo_ref, ho_ref, slots[write_slot], slots[read_slot], buf_ref, y_ref,
                       shf_ref, mix_ref, pool_ref)

    @pl.when(s % 2 == 0)
    def _():
        step(0, 1)

    @pl.when(s % 2 == 1)
    def _():
        step(1, 0)


def _tokenmix(z, x, layer, pool_w, pool_scale, conf_w, conf_b, ln_g, ln_b, sconv_w, gate_b,
              wa, wb, wc, wo, norm_g):
    tm, halo = TM_TM, TM_HALO
    ratio = tm // halo
    feat_tile = lambda s: jnp.minimum(s, TM_TILES - 1)
    mix_tile = lambda s: jnp.maximum(s - 1, 0)
    lspec = lambda shape, **kw: _layer_spec(shape, layer, 1, **kw)
    once = dict(pipeline_mode=pl.Buffered(1))
    row_spec = pl.BlockSpec((tm, D_MODEL), lambda s: (mix_tile(s), 0))
    feat = pltpu.VMEM((tm, BRANCH_WIDTH), BF16)
    est = (2 * (2 * tm + halo) * OFF_SCONV * 2 + 2 * tm * D_MODEL * (4 + 4 + 2)
           + POOL_WIDTH * POOL_GROUP_DIM * 4 + (3 * BRANCH_WIDTH + D_MODEL) * D_MODEL * 2
           + (2 * tm + halo) * BRANCH_WIDTH * 4 + (SUBLANES - 1) * TM_SHIFT_ROWS * LANES * 4
           + 6 * tm * BRANCH_WIDTH * 2 + tm * D_MODEL * 2 + 6 * tm * MIX_NC * 4)
    return pl.pallas_call(
        _tokenmix_kernel,
        grid=(TM_TILES + 1,),
        in_specs=[pl.BlockSpec((tm, OFF_SCONV), lambda s: (feat_tile(s), 0)),
                  pl.BlockSpec((halo, OFF_SCONV),
                               lambda s: (jnp.maximum(feat_tile(s) * ratio - 1, 0), 0)),
                  pl.BlockSpec((tm, GATE_WIDTH), lambda s: (mix_tile(s), OFF_SCONV // GATE_WIDTH)),
                  row_spec,
                  lspec((POOL_GROUPS, POOL_GROUP_DIM, POOL_GROUP_DIM), **once),
                  lspec((1, POOL_WIDTH)),
                  lspec((CONF_KERNEL, CONF_WIDTH)),
                  lspec((1, CONF_WIDTH)), lspec((1, CONF_WIDTH)), lspec((1, CONF_WIDTH)),
                  lspec((SCONV_KERNEL, SCONV_WIDTH)),
                  lspec((1, GATE_WIDTH)),
                  lspec((BRANCH_WIDTH, D_MODEL), **once), lspec((BRANCH_WIDTH, D_MODEL), **once),
                  lspec((BRANCH_WIDTH, D_MODEL), **once), lspec((D_MODEL, D_MODEL), **once),
                  lspec((1, D_MODEL))],
        out_specs=[row_spec, row_spec],
        out_shape=[jax.ShapeDtypeStruct((ROWS, D_MODEL), F32),
                   jax.ShapeDtypeStruct((ROWS, D_MODEL), BF16)],
        scratch_shapes=[feat, feat, feat, feat, feat, feat,
                        pltpu.VMEM((halo + tm, BRANCH_WIDTH), F32),
                        pltpu.VMEM((tm, BRANCH_WIDTH), F32),
                        pltpu.VMEM((SUBLANES - 1, TM_SHIFT_ROWS, LANES), F32),
                        pltpu.VMEM((tm, D_MODEL), BF16),
                        pltpu.VMEM((halo + tm + max(POOL_WINDOWS) // 2, POOL_GROUP_DIM), F32)],
        compiler_params=_compiler_params(("arbitrary",), est),
        name="tokenmix",
    )(z, z, z, x, pool_w, _layer_row(pool_scale), conf_w, _layer_row(conf_b), _layer_row(ln_g),
      _layer_row(ln_b), sconv_w, _layer_row(gate_b), wa, wb, wc, wo, _layer_row(norm_g))


UP_TM = 1024
UP_TN = 512
UP_ROW_TILES = ROWS // UP_TM
UP_TILES_PER_SEQ = SEQ // UP_TM
UP_COL_TILES = pl.cdiv(D_FF, UP_TN)
D_FF_PAD = UP_COL_TILES * UP_TN
UP_LAST_SHIFT = D_FF_PAD - D_FF
UP_TAIL = SUBLANES
DOWN_SLAB = 256
DOWN_SLABS_PER_COL_TILE = UP_TN // DOWN_SLAB
DOWN_SLAB_EVERY = UP_ROW_TILES // DOWN_SLABS_PER_COL_TILE
DOWN_LAST_SLAB_ROWS = D_FF - (D_FF_PAD - DOWN_SLAB)


def _up_col_start(j, offset=0):
    tile = jnp.minimum(j * (UP_TN // LANES), (D_FF - UP_TN) // LANES)
    return (offset // LANES + tile) * LANES


def _down_slab(j, i):
    return j * DOWN_SLABS_PER_COL_TILE + i // DOWN_SLAB_EVERY


def _ffn_up_kernel(h_ref, wg_ref, wv_ref, cg_ref, cv_ref, wd_ref, o_ref, wdo_ref,
                   wg_bf, wv_bf, cw_ref, ug_ref, uv_ref):
    tm, tn = UP_TM, UP_TN
    j, i = pl.program_id(0), pl.program_id(1)
    last = UP_COL_TILES - 1

    def stage_weights(shift):
        keep = tn - shift
        _cast_weight_tile(wg_ref.at[0], wg_bf, shift)
        _cast_weight_tile(wv_ref.at[0], wv_bf, shift)
        for r, ref in enumerate((cg_ref, cv_ref)):
            rows = slice(r * SUBLANES, r * SUBLANES + FFN_KERNEL)
            cw_ref[rows, 0:keep] = ref[0, :, shift:tn]
            if shift:
                cw_ref[rows, keep:tn] = jnp.zeros((FFN_KERNEL, shift), F32)

    @pl.when((i == 0) & (j != last))
    def _():
        stage_weights(0)

    @pl.when((i == 0) & (j == last))
    def _():
        stage_weights(UP_LAST_SHIFT)

    def stage_down(rows):
        wdo_ref[0:rows, :] = wd_ref[0:rows, :].astype(BF16)
        if rows < DOWN_SLAB:
            wdo_ref[rows:DOWN_SLAB, :] = jnp.zeros((DOWN_SLAB - rows, D_MODEL), BF16)

    is_visit = i % DOWN_SLAB_EVERY == 0
    is_last_slab = (j == last) & (i // DOWN_SLAB_EVERY == DOWN_SLABS_PER_COL_TILE - 1)

    @pl.when(is_visit & jnp.logical_not(is_last_slab))
    def _():
        stage_down(DOWN_SLAB)

    @pl.when(is_visit & is_last_slab)
    def _():
        stage_down(DOWN_LAST_SLAB_ROWS)

    @pl.when(i % UP_TILES_PER_SEQ == 0)
    def _():
        for u_ref in (ug_ref, uv_ref):
            for k in range(1, FFN_KERNEL):
                u_ref[k, tm:tm + k, :] = jnp.zeros((k, tn), F32)

    for u_ref in (ug_ref, uv_ref):
        for k in range(1, FFN_KERNEL):
            u_ref[k, 0:k, :] = u_ref[k, tm:tm + k, :]

    h = h_ref[...]
    for u_ref, w_bf in ((ug_ref, wg_bf), (uv_ref, wv_bf)):
        up = jnp.dot(h, w_bf[...], preferred_element_type=F32)
        for k in range(FFN_KERNEL):
            u_ref[k, k:k + tm, :] = up

    def conv(u_ref, w_row):
        acc = None
        for k in range(FFN_KERNEL):
            src = u_ref[FFN_KERNEL - 1 - k, 0:tm, :]
            term = cw_ref[w_row + k:w_row + k + 1, :] * src
            acc = term if acc is None else acc + term
        return acc

    gt = conv(ug_ref, 0)
    vl = conv(uv_ref, SUBLANES)
    o_ref[...] = (gt * _sigmoid(gt) * vl).astype(o_ref.dtype)


def _ffn_up(h, ffn_up, ffn_conv_w, ffn_down, layer):
    tm, tn = UP_TM, UP_TN
    w_spec = lambda off: pl.BlockSpec((pl.Element(1), pl.Element(D_MODEL), pl.Element(tn)),
                                      lambda j, i: (layer, 0, _up_col_start(j, off)))
    cw_spec = lambda off: pl.BlockSpec((pl.Element(1), pl.Element(FFN_KERNEL), pl.Element(tn)),
                                       lambda j, i: (layer, 0, _up_col_start(j, off)))
    u_scratch = pltpu.VMEM((FFN_KERNEL, tm + UP_TAIL, tn), F32)
    est = (2 * tm * D_MODEL * 2 + 2 * 2 * D_MODEL * tn * 4 + 2 * D_MODEL * tn * 2 + 2 * tm * tn * 2
           + 2 * FFN_KERNEL * (tm + UP_TAIL) * tn * 4 + 2 * DOWN_SLAB * D_MODEL * (4 + 2)
           + 4 * tm * tn * 4)
    return pl.pallas_call(
        _ffn_up_kernel,
        grid=(UP_COL_TILES, UP_ROW_TILES),
        in_specs=[pl.BlockSpec((tm, D_MODEL), lambda j, i: (i, 0)),
                  w_spec(0), w_spec(D_FF), cw_spec(0), cw_spec(D_FF),
                  pl.BlockSpec((None, DOWN_SLAB, D_MODEL),
                               lambda j, i: (layer, _down_slab(j, i), 0))],
        out_specs=[pl.BlockSpec((tm, tn), lambda j, i: (i, j)),
                   pl.BlockSpec((DOWN_SLAB, D_MODEL), lambda j, i: (_down_slab(j, i), 0))],
        out_shape=[jax.ShapeDtypeStruct((ROWS, D_FF_PAD), BF16),
                   jax.ShapeDtypeStruct((D_FF_PAD, D_MODEL), BF16)],
        scratch_shapes=[pltpu.VMEM((D_MODEL, tn), BF16), pltpu.VMEM((D_MODEL, tn), BF16),
                        pltpu.VMEM((2 * SUBLANES, tn), F32), u_scratch, u_scratch],
        compiler_params=_compiler_params(("arbitrary", "arbitrary"), est),
        name="ffn_up",
    )(h, ffn_up, ffn_up, ffn_conv_w, ffn_conv_w, ffn_down)


DOWN_TM = 256


def _ffn_down_kernel(act_ref, w_ref, x_ref, g_ref, *out_refs, last):
    xn = x_ref[...] + jnp.dot(act_ref[...], w_ref[...], preferred_element_type=F32)
    normed = _rms_scale(xn, g_ref[...])
    if last:
        (y_ref,) = out_refs
        y_ref[...] = normed
    else:
        xo_ref, ho_ref = out_refs
        xo_ref[...] = xn
        ho_ref[...] = normed.astype(ho_ref.dtype)


def _ffn_down(act, wd, x, g, last):
    tm = DOWN_TM
    row_spec = pl.BlockSpec((tm, D_MODEL), lambda i: (i, 0))
    if last:
        out_specs = [row_spec]
        out_shape = [jax.ShapeDtypeStruct((ROWS, D_MODEL), F32)]
    else:
        out_specs = [row_spec, row_spec]
        out_shape = [jax.ShapeDtypeStruct((ROWS, D_MODEL), F32),
                     jax.ShapeDtypeStruct((ROWS, D_MODEL), BF16)]
    est = (2 * tm * D_FF_PAD * 2 + D_FF_PAD * D_MODEL * 2 + 2 * tm * D_MODEL * (4 + 4 + 2)
           + 3 * tm * D_MODEL * 4)
    return pl.pallas_call(
        functools.partial(_ffn_down_kernel, last=last),
        grid=(ROWS // tm,),
        in_specs=[pl.BlockSpec((tm, D_FF_PAD), lambda i: (i, 0)),
                  pl.BlockSpec((D_FF_PAD, D_MODEL), lambda i: (0, 0), pipeline_mode=pl.Buffered(1)),
                  row_spec,
                  pl.BlockSpec((1, D_MODEL), lambda i: (0, 0))],
        out_specs=out_specs,
        out_shape=out_shape,
        compiler_params=_compiler_params(("arbitrary",), est),
        name="ffn_down",
    )(act, wd, x, g.reshape(1, -1))


def kernel(x, norm1_g, w_in, gate_b, pool_w, pool_scale, pool_proj, conf_conv_w, conf_conv_b,
           conf_ln_g, conf_ln_b, conf_proj, sconv_w, sconv_proj, w_o, norm2_g, ffn_up, ffn_conv_w,
           ffn_down, final_g):
    assert x.shape == (BATCH, SEQ, D_MODEL) and w_in.shape == (DEPTH, D_MODEL, D_IN)
    xs = x.reshape(ROWS, D_MODEL)
    wa, wb, wc, wo = (w.astype(BF16) for w in (pool_proj, conf_proj, sconv_proj, w_o))
    h = _rmsnorm_cast(xs, norm1_g, 0)
    for l in range(DEPTH):
        z = _inproj(h, w_in, l)
        xs, h = _tokenmix(z, xs, l, pool_w, pool_scale, conf_conv_w, conf_conv_b, conf_ln_g,
                          conf_ln_b, sconv_w, gate_b, wa, wb, wc, wo, norm2_g)
        act, wd = _ffn_up(h, ffn_up, ffn_conv_w, ffn_down, l)
        last = l == DEPTH - 1
        g_next = final_g if last else norm1_g[l + 1]
        outs = _ffn_down(act, wd, xs, g_next, last)
        if last:
            (y,) = outs
        else:
            xs, h = outs
    return y.reshape(BATCH, SEQ, D_MODEL)
```

```python
import functools

import jax
import jax.numpy as jnp
from jax import lax
from jax.experimental import pallas as pl
from jax.experimental.pallas import tpu as pltpu

D_MODEL = 2048
BATCH = 4
SEQ = 2048
DEPTH = 2
ROWS = BATCH * SEQ

POOL_WIDTH = 1024
POOL_GROUPS = 4
POOL_WINDOWS = (2, 4, 8, 16)
POOL_GROUP_DIM = POOL_WIDTH // POOL_GROUPS
CONF_WIDTH = 1024
CONF_KERNEL = 31
SCONV_WIDTH = 1024
SCONV_KERNEL = 3
N_BRANCH = 3
BRANCH_WIDTH = 1024
OFF_POOL = POOL_WIDTH
OFF_CONF = OFF_POOL + 2 * CONF_WIDTH
OFF_SCONV = OFF_CONF + 3 * SCONV_WIDTH
GATE_WIDTH = N_BRANCH * D_MODEL
D_IN = OFF_SCONV + GATE_WIDTH
D_FF = 5504
FFN_KERNEL = 3
EPS = 1e-6

V7X_VMEM_BYTES = 64 * 1024 * 1024
SUBLANES = 8
LANES = 128

BF16 = jnp.bfloat16
F32 = jnp.float32


def _compiler_params(semantics, vmem_estimate_bytes):
    limit = min(int(vmem_estimate_bytes * 1.25) + (4 << 20), V7X_VMEM_BYTES - (4 << 20))
    return pltpu.CompilerParams(dimension_semantics=semantics, vmem_limit_bytes=limit)


def _sigmoid(x):
    return 1.0 / (1.0 + jnp.exp(-x))


def _rms_scale(x, g):
    ms = jnp.mean(x * x, axis=-1, keepdims=True)
    return x * lax.rsqrt(ms + EPS) * g


def _layer_row(stacked):
    return stacked.reshape(stacked.shape[0], 1, stacked.shape[1])


def _layer_spec(shape, layer, grid_rank, **kwargs):
    zeros = (0,) * len(shape)
    if grid_rank == 1:
        index_map = lambda i: (layer,) + zeros
    else:
        index_map = lambda j, i: (layer,) + zeros
    return pl.BlockSpec((None,) + tuple(shape), index_map, **kwargs)


NORM_TM = 512


def _rmsnorm_kernel(x_ref, g_ref, o_ref):
    o_ref[...] = _rms_scale(x_ref[...], g_ref[...]).astype(o_ref.dtype)


def _rmsnorm_cast(x, g, layer):
    tm = NORM_TM
    est = 2 * tm * D_MODEL * (4 + 2) + 4 * tm * D_MODEL * 4
    return pl.pallas_call(
        _rmsnorm_kernel,
        grid=(ROWS // tm,),
        in_specs=[pl.BlockSpec((tm, D_MODEL), lambda i: (i, 0)),
                  _layer_spec((1, D_MODEL), layer, 1)],
        out_specs=pl.BlockSpec((tm, D_MODEL), lambda i: (i, 0)),
        out_shape=jax.ShapeDtypeStruct((ROWS, D_MODEL), BF16),
        compiler_params=_compiler_params(("arbitrary",), est),
        name="rmsnorm_cast",
    )(x, _layer_row(g))


CONV_HALO = 32
CONV_ROW_BLOCK = 256


def _shifted_causal_conv(buf_ref, shf_ref, cw_ref, bias, cols, taps, tm, out_ref):
    base = CONV_HALO - (taps - 1)
    span = CONV_HALO + tm - SUBLANES
    residues = sorted({(base + k) % SUBLANES for k in range(taps)} - {0})
    for r in residues:
        shf_ref[r - 1, :, :] = buf_ref[r:r + span, cols]
    for lo in range(0, tm, CONV_ROW_BLOCK):
        acc = bias
        for k in range(taps):
            a, r = divmod(base + k, SUBLANES)
            row = a * SUBLANES + lo
            if r == 0:
                src = buf_ref[row:row + CONV_ROW_BLOCK, cols]
            else:
                src = shf_ref[r - 1, row:row + CONV_ROW_BLOCK, :]
            acc = acc + cw_ref[k:k + 1, cols] * src
        out_ref[lo:lo + CONV_ROW_BLOCK, cols] = acc


INPROJ_TM = 1024
INPROJ_TN = 1024
INPROJ_HALF = INPROJ_TN // 2
INPROJ_TILES_PER_SEQ = SEQ // INPROJ_TM
CONF_COL_TILES = CONF_WIDTH // INPROJ_HALF
CONF_FIRST_TILE = OFF_POOL // INPROJ_TN
CAST_ROWS = 256


def _cast_weight_tile(w_ref, wbf_ref, lane_shift=0):
    rows, cols = wbf_ref.shape
    keep = cols - lane_shift

    def body(r, carry):
        sl = pl.ds(pl.multiple_of(r * CAST_ROWS, CAST_ROWS), CAST_ROWS)
        wbf_ref[sl, 0:keep] = w_ref[sl, lane_shift:cols].astype(BF16)
        if lane_shift:
            wbf_ref[sl, keep:cols] = jnp.zeros((CAST_ROWS, lane_shift), BF16)
        return carry

    lax.fori_loop(0, rows // CAST_ROWS, body, 0)


def _cast_weight_pairs(wlo_ref, whi_ref, wbf_ref):
    rows = wbf_ref.shape[0]

    def body(r, carry):
        sl = pl.ds(pl.multiple_of(r * CAST_ROWS, CAST_ROWS), CAST_ROWS)
        for c in range(INPROJ_HALF // LANES):
            src = slice(c * LANES, (c + 1) * LANES)
            wbf_ref[sl, 2 * c * LANES:(2 * c + 1) * LANES] = wlo_ref[sl, src].astype(BF16)
            wbf_ref[sl, (2 * c + 1) * LANES:(2 * c + 2) * LANES] = whi_ref[sl, src].astype(BF16)
        return carry

    lax.fori_loop(0, rows // CAST_ROWS, body, 0)


def _inproj_kernel(h_ref, wlo_ref, whi_ref, cw_ref, cb_ref, z_ref, y_ref, wbf_ref, v_ref, shf_ref):
    tm, half = INPROJ_TM, INPROJ_HALF
    j, i = pl.program_id(0), pl.program_id(1)
    is_conf = (j >= CONF_FIRST_TILE) & (j < CONF_FIRST_TILE + CONF_COL_TILES)

    @pl.when(jnp.logical_not(is_conf))
    def _():
        @pl.when(i == 0)
        def _():
            _cast_weight_tile(wlo_ref, wbf_ref.at[:, 0:half])
            _cast_weight_tile(whi_ref, wbf_ref.at[:, half:2 * half])

        z_ref[...] = jnp.dot(h_ref[...], wbf_ref[...],
                             preferred_element_type=F32).astype(z_ref.dtype)

    @pl.when(is_conf)
    def _():
        @pl.when(i == 0)
        def _():
            _cast_weight_pairs(wlo_ref, whi_ref, wbf_ref)

        @pl.when(i % INPROJ_TILES_PER_SEQ == 0)
        def _():
            v_ref[0:CONV_HALO, :] = jnp.zeros((CONV_HALO, half), F32)

        z_ref[...] = jnp.zeros(z_ref.shape, z_ref.dtype)
        h = h_ref[...]
        for c in range(half // LANES):
            cols = slice(c * LANES, (c + 1) * LANES)
            pair = jnp.dot(h, wbf_ref[:, 2 * c * LANES:(2 * c + 2) * LANES],
                           preferred_element_type=F32)
            v_ref[CONV_HALO:, cols] = pair[:, 0:LANES] * _sigmoid(pair[:, LANES:2 * LANES])
            _shifted_causal_conv(v_ref, shf_ref, cw_ref, cb_ref[:, cols], cols, CONF_KERNEL, tm,
                                 y_ref)
        v_ref[0:CONV_HALO, :] = v_ref[tm:tm + CONV_HALO, :]


def _inproj(h, w_in, conf_w, conf_b, layer):
    tm, tn, half = INPROJ_TM, INPROJ_TN, INPROJ_HALF
    row_tiles = ROWS // tm
    conf = lambda j: jnp.clip(j - CONF_FIRST_TILE, 0, CONF_COL_TILES - 1)
    is_conf = lambda j: (j >= CONF_FIRST_TILE) & (j < CONF_FIRST_TILE + CONF_COL_TILES)
    val0, gate0 = OFF_POOL // half, (OFF_POOL + CONF_WIDTH) // half
    lo_block = lambda j: jnp.where(is_conf(j), val0 + conf(j), 2 * j)
    hi_block = lambda j: jnp.where(is_conf(j), gate0 + conf(j), 2 * j + 1)

    def y_index(j, i):
        row = jnp.where(j < CONF_FIRST_TILE, 0,
                        jnp.where(j >= CONF_FIRST_TILE + CONF_COL_TILES, row_tiles - 1, i))
        return (row, conf(j))

    est = (2 * tm * D_MODEL * 2 + 2 * 2 * D_MODEL * half * 4 + D_MODEL * tn * 2 + 2 * tm * tn * 2
           + 2 * tm * half * 4 + (CONV_HALO + tm) * half * 4
           + (SUBLANES - 1) * (CONV_HALO + tm) * LANES * 4 + 4 * tm * 2 * LANES * 4)
    return pl.pallas_call(
        _inproj_kernel,
        grid=(D_IN // tn, row_tiles),
        in_specs=[pl.BlockSpec((tm, D_MODEL), lambda j, i: (i, 0)),
                  pl.BlockSpec((None, D_MODEL, half), lambda j, i: (layer, 0, lo_block(j))),
                  pl.BlockSpec((None, D_MODEL, half), lambda j, i: (layer, 0, hi_block(j))),
                  pl.BlockSpec((None, CONF_KERNEL, half), lambda j, i: (layer, 0, conf(j))),
                  pl.BlockSpec((None, 1, half), lambda j, i: (layer, 0, conf(j)))],
        out_specs=[pl.BlockSpec((tm, tn), lambda j, i: (i, j)),
                   pl.BlockSpec((tm, half), y_index)],
        out_shape=[jax.ShapeDtypeStruct((ROWS, D_IN), BF16),
                   jax.ShapeDtypeStruct((ROWS, CONF_WIDTH), F32)],
        scratch_shapes=[pltpu.VMEM((D_MODEL, tn), BF16),
                        pltpu.VMEM((CONV_HALO + tm, half), F32),
                        pltpu.VMEM((SUBLANES - 1, CONV_HALO + tm - SUBLANES, LANES), F32)],
        compiler_params=_compiler_params(("arbitrary", "arbitrary"), est),
        name="inproj",
    )(h, w_in, w_in, conf_w, _layer_row(conf_b))


TM_TM = 256
TM_HALO = 32
TM_TILES = ROWS // TM_TM
TM_TILES_PER_SEQ = SEQ // TM_TM
MIX_NC = 512
SCONV_COLS = 3 * SCONV_WIDTH


def _tokenmix_step(t_idx, up_ref, upp_ref, us_ref, usp_ref, y_ref, zg_ref, x_ref, pw_ref, ps_ref,
                   lg_ref, lb_ref, sw_ref, gb_ref, wa_ref, wb_ref, wc_ref, wo_ref, g_ref,
                   xo_ref, ho_ref, write_feats, read_feats, buf_ref, mix_ref, pool_ref):
    tm, halo = TM_TM, TM_HALO
    fa_ref, fb_ref, fc_ref = write_feats
    keep = jnp.where(t_idx == 0, 0.0, 1.0).astype(F32)

    for c in range(D_MODEL // MIX_NC):
        cols = slice(c * MIX_NC, (c + 1) * MIX_NC)
        acc = None
        for k, (f_ref, w_ref) in enumerate(zip(read_feats, (wa_ref, wb_ref, wc_ref))):
            gcols = slice(k * D_MODEL + c * MIX_NC, k * D_MODEL + (c + 1) * MIX_NC)
            gate = _sigmoid(zg_ref[:, gcols].astype(F32) + gb_ref[:, gcols])
            term = gate * jnp.dot(f_ref[...], w_ref[:, cols], preferred_element_type=F32)
            acc = term if acc is None else acc + term
        mix_ref[:, cols] = acc.astype(mix_ref.dtype)

    buf_ref[0:halo, :] = upp_ref[...].astype(F32) * keep
    buf_ref[halo:, :] = up_ref[...].astype(F32)
    pos = (t_idx * tm + 1 + lax.broadcasted_iota(jnp.int32, (tm, 1), 0)).astype(F32)
    ext = halo + tm
    pool_ref[0:SUBLANES, :] = jnp.zeros((SUBLANES, POOL_GROUP_DIM), F32)
    for g, w in enumerate(POOL_WINDOWS):
        cols = slice(g * POOL_GROUP_DIM, (g + 1) * POOL_GROUP_DIM)
        ws = buf_ref[:, cols]
        x = ws[halo:, :]
        d = 1
        while d < w:
            pool_ref[d:d + ext, :] = ws
            ws = ws + pool_ref[0:ext, :]
            d *= 2
        inv_count = 1.0 / jnp.minimum(pos, float(w))
        pooled = ws[halo:, :] * inv_count - x
        mixed = jnp.dot(pooled.astype(BF16), pw_ref[g].astype(BF16), preferred_element_type=F32)
        fa_ref[:, cols] = (mixed * ps_ref[:, cols]).astype(fa_ref.dtype)

    def sc(ref, part):
        return ref[:, part * SCONV_WIDTH:(part + 1) * SCONV_WIDTH].astype(F32)

    buf_ref[0:halo, :] = sc(usp_ref, 1) * sc(usp_ref, 2) * keep
    buf_ref[halo:, :] = sc(us_ref, 1) * sc(us_ref, 2)
    base = halo - (SCONV_KERNEL - 1)
    q = sw_ref[0:1, :] * buf_ref[base:base + tm, :]
    for k in range(1, SCONV_KERNEL):
        q = q + sw_ref[k:k + 1, :] * buf_ref[base + k:base + k + tm, :]
    fc_ref[...] = (sc(us_ref, 0) * q).astype(fc_ref.dtype)

    for c in range(D_MODEL // MIX_NC):
        cols = slice(c * MIX_NC, (c + 1) * MIX_NC)
        xo_ref[:, cols] = x_ref[:, cols] + jnp.dot(mix_ref[...], wo_ref[:, cols],
                                                   preferred_element_type=F32)

    y = y_ref[...]
    mu = jnp.mean(y, axis=-1, keepdims=True)
    yc = y - mu
    var = jnp.mean(yc * yc, axis=-1, keepdims=True)
    yn = yc * lax.rsqrt(var + EPS) * lg_ref[...] + lb_ref[...]
    fb_ref[...] = (yn * _sigmoid(yn)).astype(fb_ref.dtype)
    ho_ref[...] = _rms_scale(xo_ref[...], g_ref[...]).astype(ho_ref.dtype)


def _tokenmix_kernel(up_ref, upp_ref, us_ref, usp_ref, y_ref, zg_ref, x_ref, pw_ref, ps_ref, lg_ref,
                     lb_ref, sw_ref, gb_ref, wa_ref, wb_ref, wc_ref, wo_ref, g_ref, xo_ref, ho_ref,
                     a0_ref, b0_ref, c0_ref, a1_ref, b1_ref, c1_ref, buf_ref, mix_ref, pool_ref):
    s = pl.program_id(0)
    t_idx = jnp.minimum(s, TM_TILES - 1) % TM_TILES_PER_SEQ
    slots = ((a0_ref, b0_ref, c0_ref), (a1_ref, b1_ref, c1_ref))

    @pl.when(s == 0)
    def _():
        for ref in slots[1]:
            ref[...] = jnp.zeros(ref.shape, ref.dtype)

    def step(write_slot, read_slot):
        _tokenmix_step(t_idx, up_ref, upp_ref, us_ref, usp_ref, y_ref, zg_ref, x_ref, pw_ref,
                       ps_ref, lg_ref, lb_ref, sw_ref, gb_ref, wa_ref, wb_ref, wc_ref, wo_ref,
                       g_ref, xo_ref, ho_ref, slots[write_slot], slots[read_slot], buf_ref,
                       mix_ref, pool_ref)

    @pl.when(s % 2 == 0)
    def _():
        step(0, 1)

    @pl.when(s % 2 == 1)
    def _():
        step(1, 0)


def _tokenmix(z, y, x, layer, pool_w, pool_scale, ln_g, ln_b, sconv_w, gate_b, wa, wb, wc, wo,
              norm_g):
    tm, halo = TM_TM, TM_HALO
    ratio = tm // halo
    feat_tile = lambda s: jnp.minimum(s, TM_TILES - 1)
    halo_tile = lambda s: jnp.maximum(feat_tile(s) * ratio - 1, 0)
    mix_tile = lambda s: jnp.maximum(s - 1, 0)
    lspec = lambda shape, **kw: _layer_spec(shape, layer, 1, **kw)
    once = dict(pipeline_mode=pl.Buffered(1))
    row_spec = pl.BlockSpec((tm, D_MODEL), lambda s: (mix_tile(s), 0))
    feat = pltpu.VMEM((tm, BRANCH_WIDTH), BF16)
    est = (2 * (tm + halo) * (POOL_WIDTH + SCONV_COLS) * 2 + 2 * tm * CONF_WIDTH * 4
           + 2 * tm * GATE_WIDTH * 2 + 2 * tm * D_MODEL * (4 + 4 + 2)
           + POOL_WIDTH * POOL_GROUP_DIM * 4 + (3 * BRANCH_WIDTH + D_MODEL) * D_MODEL * 2
           + (tm + halo) * BRANCH_WIDTH * 4 + 6 * tm * BRANCH_WIDTH * 2 + tm * D_MODEL * 2
           + 8 * tm * MIX_NC * 4)
    return pl.pallas_call(
        _tokenmix_kernel,
        grid=(TM_TILES + 1,),
        in_specs=[pl.BlockSpec((tm, POOL_WIDTH), lambda s: (feat_tile(s), 0)),
                  pl.BlockSpec((halo, POOL_WIDTH), lambda s: (halo_tile(s), 0)),
                  pl.BlockSpec((tm, SCONV_COLS), lambda s: (feat_tile(s), OFF_CONF // SCONV_COLS)),
                  pl.BlockSpec((halo, SCONV_COLS), lambda s: (halo_tile(s), OFF_CONF // SCONV_COLS)),
                  pl.BlockSpec((tm, CONF_WIDTH), lambda s: (feat_tile(s), 0)),
                  pl.BlockSpec((tm, GATE_WIDTH), lambda s: (mix_tile(s), OFF_SCONV // GATE_WIDTH)),
                  row_spec,
                  lspec((POOL_GROUPS, POOL_GROUP_DIM, POOL_GROUP_DIM), **once),
                  lspec((1, POOL_WIDTH)),
                  lspec((1, CONF_WIDTH)), lspec((1, CONF_WIDTH)),
                  lspec((SCONV_KERNEL, SCONV_WIDTH)),
                  lspec((1, GATE_WIDTH)),
                  lspec((BRANCH_WIDTH, D_MODEL), **once), lspec((BRANCH_WIDTH, D_MODEL), **once),
                  lspec((BRANCH_WIDTH, D_MODEL), **once), lspec((D_MODEL, D_MODEL), **once),
                  lspec((1, D_MODEL))],
        out_specs=[row_spec, row_spec],
        out_shape=[jax.ShapeDtypeStruct((ROWS, D_MODEL), F32),
                   jax.ShapeDtypeStruct((ROWS, D_MODEL), BF16)],
        scratch_shapes=[feat, feat, feat, feat, feat, feat,
                        pltpu.VMEM((halo + tm, BRANCH_WIDTH), F32),
                        pltpu.VMEM((tm, D_MODEL), BF16),
                        pltpu.VMEM((halo + tm + max(POOL_WINDOWS) // 2, POOL_GROUP_DIM), F32)],
        compiler_params=_compiler_params(("arbitrary",), est),
        name="tokenmix",
    )(z, z, z, z, y, z, x, pool_w, _layer_row(pool_scale), _layer_row(ln_g), _layer_row(ln_b),
      sconv_w, _layer_row(gate_b), wa, wb, wc, wo, _layer_row(norm_g))


UP_TM = 1024
UP_TN = 512
UP_ROW_TILES = ROWS // UP_TM
UP_TILES_PER_SEQ = SEQ // UP_TM
UP_COL_TILES = pl.cdiv(D_FF, UP_TN)
D_FF_PAD = UP_COL_TILES * UP_TN
UP_LAST_SHIFT = D_FF_PAD - D_FF
UP_TAIL = SUBLANES
DOWN_SLAB = 256
DOWN_SLABS_PER_COL_TILE = UP_TN // DOWN_SLAB
DOWN_SLAB_EVERY = UP_ROW_TILES // DOWN_SLABS_PER_COL_TILE
DOWN_LAST_SLAB_ROWS = D_FF - (D_FF_PAD - DOWN_SLAB)


def _up_col_start(j, offset=0):
    tile = jnp.minimum(j * (UP_TN // LANES), (D_FF - UP_TN) // LANES)
    return (offset // LANES + tile) * LANES


def _down_slab(j, i):
    return j * DOWN_SLABS_PER_COL_TILE + i // DOWN_SLAB_EVERY


def _ffn_up_kernel(h_ref, wg_ref, wv_ref, cg_ref, cv_ref, wd_ref, o_ref, wdo_ref,
                   wg_bf, wv_bf, cw_ref, ug_ref, uv_ref):
    tm, tn = UP_TM, UP_TN
    j, i = pl.program_id(0), pl.program_id(1)
    last = UP_COL_TILES - 1

    def stage_weights(shift):
        keep = tn - shift
        _cast_weight_tile(wg_ref.at[0], wg_bf, shift)
        _cast_weight_tile(wv_ref.at[0], wv_bf, shift)
        for r, ref in enumerate((cg_ref, cv_ref)):
            rows = slice(r * SUBLANES, r * SUBLANES + FFN_KERNEL)
            cw_ref[rows, 0:keep] = ref[0, :, shift:tn]
            if shift:
                cw_ref[rows, keep:tn] = jnp.zeros((FFN_KERNEL, shift), F32)

    @pl.when((i == 0) & (j != last))
    def _():
        stage_weights(0)

    @pl.when((i == 0) & (j == last))
    def _():
        stage_weights(UP_LAST_SHIFT)

    def stage_down(rows):
        wdo_ref[0:rows, :] = wd_ref[0:rows, :].astype(BF16)
        if rows < DOWN_SLAB:
            wdo_ref[rows:DOWN_SLAB, :] = jnp.zeros((DOWN_SLAB - rows, D_MODEL), BF16)

    is_visit = i % DOWN_SLAB_EVERY == 0
    is_last_slab = (j == last) & (i // DOWN_SLAB_EVERY == DOWN_SLABS_PER_COL_TILE - 1)

    @pl.when(is_visit & jnp.logical_not(is_last_slab))
    def _():
        stage_down(DOWN_SLAB)

    @pl.when(is_visit & is_last_slab)
    def _():
        stage_down(DOWN_LAST_SLAB_ROWS)

    @pl.when(i % UP_TILES_PER_SEQ == 0)
    def _():
        for u_ref in (ug_ref, uv_ref):
            for k in range(1, FFN_KERNEL):
                u_ref[k, tm:tm + k, :] = jnp.zeros((k, tn), F32)

    for u_ref in (ug_ref, uv_ref):
        for k in range(1, FFN_KERNEL):
            u_ref[k, 0:k, :] = u_ref[k, tm:tm + k, :]

    h = h_ref[...]
    for u_ref, w_bf in ((ug_ref, wg_bf), (uv_ref, wv_bf)):
        up = jnp.dot(h, w_bf[...], preferred_element_type=F32)
        for k in range(FFN_KERNEL):
            u_ref[k, k:k + tm, :] = up

    def conv(u_ref, w_row):
        acc = None
        for k in range(FFN_KERNEL):
            src = u_ref[FFN_KERNEL - 1 - k, 0:tm, :]
            term = cw_ref[w_row + k:w_row + k + 1, :] * src
            acc = term if acc is None else acc + term
        return acc

    gt = conv(ug_ref, 0)
    vl = conv(uv_ref, SUBLANES)
    o_ref[...] = (gt * _sigmoid(gt) * vl).astype(o_ref.dtype)


def _ffn_up(h, ffn_up, ffn_conv_w, ffn_down, layer):
    tm, tn = UP_TM, UP_TN
    w_spec = lambda off: pl.BlockSpec((pl.Element(1), pl.Element(D_MODEL), pl.Element(tn)),
                                      lambda j, i: (layer, 0, _up_col_start(j, off)))
    cw_spec = lambda off: pl.BlockSpec((pl.Element(1), pl.Element(FFN_KERNEL), pl.Element(tn)),
                                       lambda j, i: (layer, 0, _up_col_start(j, off)))
    u_scratch = pltpu.VMEM((FFN_KERNEL, tm + UP_TAIL, tn), F32)
    est = (2 * tm * D_MODEL * 2 + 2 * 2 * D_MODEL * tn * 4 + 2 * D_MODEL * tn * 2 + 2 * tm * tn * 2
           + 2 * FFN_KERNEL * (tm + UP_TAIL) * tn * 4 + 2 * DOWN_SLAB * D_MODEL * (4 + 2)
           + 4 * tm * tn * 4)
    return pl.pallas_call(
        _ffn_up_kernel,
        grid=(UP_COL_TILES, UP_ROW_TILES),
        in_specs=[pl.BlockSpec((tm, D_MODEL), lambda j, i: (i, 0)),
                  w_spec(0), w_spec(D_FF), cw_spec(0), cw_spec(D_FF),
                  pl.BlockSpec((None, DOWN_SLAB, D_MODEL),
                               lambda j, i: (layer, _down_slab(j, i), 0))],
        out_specs=[pl.BlockSpec((tm, tn), lambda j, i: (i, j)),
                   pl.BlockSpec((DOWN_SLAB, D_MODEL), lambda j, i: (_down_slab(j, i), 0))],
        out_shape=[jax.ShapeDtypeStruct((ROWS, D_FF_PAD), BF16),
                   jax.ShapeDtypeStruct((D_FF_PAD, D_MODEL), BF16)],
        scratch_shapes=[pltpu.VMEM((D_MODEL, tn), BF16), pltpu.VMEM((D_MODEL, tn), BF16),
                        pltpu.VMEM((2 * SUBLANES, tn), F32), u_scratch, u_scratch],
        compiler_params=_compiler_params(("arbitrary", "arbitrary"), est),
        name="ffn_up",
    )(h, ffn_up, ffn_up, ffn_conv_w, ffn_conv_w, ffn_down)


DOWN_TM = 256


def _ffn_down_kernel(act_ref, w_ref, x_ref, g_ref, *out_refs, last):
    xn = x_ref[...] + jnp.dot(act_ref[...], w_ref[...], preferred_element_type=F32)
    normed = _rms_scale(xn, g_ref[...])
    if last:
        (y_ref,) = out_refs
        y_ref[...] = normed
    else:
        xo_ref, ho_ref = out_refs
        xo_ref[...] = xn
        ho_ref[...] = normed.astype(ho_ref.dtype)


def _ffn_down(act, wd, x, g, last):
    tm = DOWN_TM
    row_spec = pl.BlockSpec((tm, D_MODEL), lambda i: (i, 0))
    if last:
        out_specs = [row_spec]
        out_shape = [jax.ShapeDtypeStruct((ROWS, D_MODEL), F32)]
    else:
        out_specs = [row_spec, row_spec]
        out_shape = [jax.ShapeDtypeStruct((ROWS, D_MODEL), F32),
                     jax.ShapeDtypeStruct((ROWS, D_MODEL), BF16)]
    est = (2 * tm * D_FF_PAD * 2 + D_FF_PAD * D_MODEL * 2 + 2 * tm * D_MODEL * (4 + 4 + 2)
           + 3 * tm * D_MODEL * 4)
    return pl.pallas_call(
        functools.partial(_ffn_down_kernel, last=last),
        grid=(ROWS // tm,),
        in_specs=[pl.BlockSpec((tm, D_FF_PAD), lambda i: (i, 0)),
                  pl.BlockSpec((D_FF_PAD, D_MODEL), lambda i: (0, 0), pipeline_mode=pl.Buffered(1)),
                  row_spec,
                  pl.BlockSpec((1, D_MODEL), lambda i: (0, 0))],
        out_specs=out_specs,
        out_shape=out_shape,
        compiler_params=_compiler_params(("arbitrary",), est),
        name="ffn_down",
    )(act, wd, x, g.reshape(1, -1))


def kernel(x, norm1_g, w_in, gate_b, pool_w, pool_scale, pool_proj, conf_conv_w, conf_conv_b,
           conf_ln_g, conf_ln_b, conf_proj, sconv_w, sconv_proj, w_o, norm2_g, ffn_up, ffn_conv_w,
           ffn_down, final_g):
    assert x.shape == (BATCH, SEQ, D_MODEL) and w_in.shape == (DEPTH, D_MODEL, D_IN)
    xs = x.reshape(ROWS, D_MODEL)
    wa, wb, wc, wo = (w.astype(BF16) for w in (pool_proj, conf_proj, sconv_proj, w_o))
    h = _rmsnorm_cast(xs, norm1_g, 0)
    for l in range(DEPTH):
        z, conv_y = _inproj(h, w_in, conf_conv_w, conf_conv_b, l)
        xs, h = _tokenmix(z, conv_y, xs, l, pool_w, pool_scale, conf_ln_g, conf_ln_b, sconv_w, gate_b,
                          wa, wb, wc, wo, norm2_g)
        act, wd = _ffn_up(h, ffn_up, ffn_conv_w, ffn_down, l)
        last = l == DEPTH - 1
        g_next = final_g if last else norm1_g[l + 1]
        outs = _ffn_down(act, wd, xs, g_next, last)
        if last:
            (y,) = outs
        else:
            xs, h = outs
    return y.reshape(BATCH, SEQ, D_MODEL)
```

```python
import functools

import jax
import jax.numpy as jnp
from jax import lax
from jax.experimental import pallas as pl
from jax.experimental.pallas import tpu as pltpu

D_MODEL = 2048
BATCH = 4
SEQ = 2048
DEPTH = 2
ROWS = BATCH * SEQ

POOL_WIDTH = 1024
POOL_GROUPS = 4
POOL_WINDOWS = (2, 4, 8, 16)
POOL_GROUP_DIM = POOL_WIDTH // POOL_GROUPS
CONF_WIDTH = 1024
CONF_KERNEL = 31
SCONV_WIDTH = 1024
SCONV_KERNEL = 3
N_BRANCH = 3
BRANCH_WIDTH = 1024
OFF_POOL = POOL_WIDTH
OFF_CONF = OFF_POOL + 2 * CONF_WIDTH
OFF_SCONV = OFF_CONF + 3 * SCONV_WIDTH
GATE_WIDTH = N_BRANCH * D_MODEL
D_IN = OFF_SCONV + GATE_WIDTH
D_FF = 5504
FFN_KERNEL = 3
EPS = 1e-6

V7X_VMEM_BYTES = 64 * 1024 * 1024
SUBLANES = 8
LANES = 128

BF16 = jnp.bfloat16
F32 = jnp.float32


def _compiler_params(semantics, vmem_estimate_bytes):
    limit = min(int(vmem_estimate_bytes * 1.25) + (4 << 20), V7X_VMEM_BYTES - (4 << 20))
    return pltpu.CompilerParams(dimension_semantics=semantics, vmem_limit_bytes=limit)


def _sigmoid(x):
    return 1.0 / (1.0 + jnp.exp(-x))


def _rms_scale(x, g):
    ms = jnp.mean(x * x, axis=-1, keepdims=True)
    return x * lax.rsqrt(ms + EPS) * g


def _layer_row(stacked):
    return stacked.reshape(stacked.shape[0], 1, stacked.shape[1])


def _layer_spec(shape, layer, grid_rank, **kwargs):
    zeros = (0,) * len(shape)
    if grid_rank == 1:
        index_map = lambda i: (layer,) + zeros
    else:
        index_map = lambda j, i: (layer,) + zeros
    return pl.BlockSpec((None,) + tuple(shape), index_map, **kwargs)


NORM_TM = 512


def _rmsnorm_kernel(x_ref, g_ref, o_ref):
    o_ref[...] = _rms_scale(x_ref[...], g_ref[...]).astype(o_ref.dtype)


def _rmsnorm_cast(x, g, layer):
    tm = NORM_TM
    est = 2 * tm * D_MODEL * (4 + 2) + 4 * tm * D_MODEL * 4
    return pl.pallas_call(
        _rmsnorm_kernel,
        grid=(ROWS // tm,),
        in_specs=[pl.BlockSpec((tm, D_MODEL), lambda i: (i, 0)),
                  _layer_spec((1, D_MODEL), layer, 1)],
        out_specs=pl.BlockSpec((tm, D_MODEL), lambda i: (i, 0)),
        out_shape=jax.ShapeDtypeStruct((ROWS, D_MODEL), BF16),
        compiler_params=_compiler_params(("arbitrary",), est),
        name="rmsnorm_cast",
    )(x, _layer_row(g))


CONV_HALO = 32
CONV_ROW_BLOCK = 256


def _shifted_causal_conv(buf_ref, shf_ref, cw_ref, bias, cols, taps, tm, out_ref):
    base = CONV_HALO - (taps - 1)
    span = CONV_HALO + tm - SUBLANES
    residues = sorted({(base + k) % SUBLANES for k in range(taps)} - {0})
    for r in residues:
        shf_ref[r - 1, :, :] = buf_ref[r:r + span, cols]
    for lo in range(0, tm, CONV_ROW_BLOCK):
        acc = bias
        for k in range(taps):
            a, r = divmod(base + k, SUBLANES)
            row = a * SUBLANES + lo
            if r == 0:
                src = buf_ref[row:row + CONV_ROW_BLOCK, cols]
            else:
                src = shf_ref[r - 1, row:row + CONV_ROW_BLOCK, :]
            acc = acc + cw_ref[k:k + 1, cols] * src
        out_ref[lo:lo + CONV_ROW_BLOCK, cols] = acc


INPROJ_TM = 1024
INPROJ_TN = 1024
INPROJ_HALF = INPROJ_TN // 2
INPROJ_TILES_PER_SEQ = SEQ // INPROJ_TM
CONF_COL_TILES = CONF_WIDTH // INPROJ_HALF
CONF_FIRST_TILE = OFF_POOL // INPROJ_TN
CAST_ROWS = 256


def _cast_weight_tile(w_ref, wbf_ref, lane_shift=0):
    rows, cols = wbf_ref.shape
    keep = cols - lane_shift

    def body(r, carry):
        sl = pl.ds(pl.multiple_of(r * CAST_ROWS, CAST_ROWS), CAST_ROWS)
        wbf_ref[sl, 0:keep] = w_ref[sl, lane_shift:cols].astype(BF16)
        if lane_shift:
            wbf_ref[sl, keep:cols] = jnp.zeros((CAST_ROWS, lane_shift), BF16)
        return carry

    lax.fori_loop(0, rows // CAST_ROWS, body, 0)


def _cast_weight_pairs(wlo_ref, whi_ref, wbf_ref):
    rows = wbf_ref.shape[0]

    def body(r, carry):
        sl = pl.ds(pl.multiple_of(r * CAST_ROWS, CAST_ROWS), CAST_ROWS)
        for c in range(INPROJ_HALF // LANES):
            src = slice(c * LANES, (c + 1) * LANES)
            wbf_ref[sl, 2 * c * LANES:(2 * c + 1) * LANES] = wlo_ref[sl, src].astype(BF16)
            wbf_ref[sl, (2 * c + 1) * LANES:(2 * c + 2) * LANES] = whi_ref[sl, src].astype(BF16)
        return carry

    lax.fori_loop(0, rows // CAST_ROWS, body, 0)


def _inproj_kernel(h_ref, wlo_ref, whi_ref, cw_ref, cb_ref, z_ref, y_ref, wbf_ref, v_ref, shf_ref):
    tm, half = INPROJ_TM, INPROJ_HALF
    j, i = pl.program_id(0), pl.program_id(1)
    is_conf = (j >= CONF_FIRST_TILE) & (j < CONF_FIRST_TILE + CONF_COL_TILES)

    @pl.when(jnp.logical_not(is_conf))
    def _():
        @pl.when(i == 0)
        def _():
            _cast_weight_tile(wlo_ref, wbf_ref.at[:, 0:half])
            _cast_weight_tile(whi_ref, wbf_ref.at[:, half:2 * half])

        z_ref[...] = jnp.dot(h_ref[...], wbf_ref[...],
                             preferred_element_type=F32).astype(z_ref.dtype)

    @pl.when(is_conf)
    def _():
        @pl.when(i == 0)
        def _():
            _cast_weight_pairs(wlo_ref, whi_ref, wbf_ref)

        @pl.when(i % INPROJ_TILES_PER_SEQ == 0)
        def _():
            v_ref[0:CONV_HALO, :] = jnp.zeros((CONV_HALO, half), F32)

        z_ref[...] = jnp.zeros(z_ref.shape, z_ref.dtype)
        h = h_ref[...]
        for c in range(half // LANES):
            cols = slice(c * LANES, (c + 1) * LANES)
            pair = jnp.dot(h, wbf_ref[:, 2 * c * LANES:(2 * c + 2) * LANES],
                           preferred_element_type=F32)
            v_ref[CONV_HALO:, cols] = pair[:, 0:LANES] * _sigmoid(pair[:, LANES:2 * LANES])
            _shifted_causal_conv(v_ref, shf_ref, cw_ref, cb_ref[:, cols], cols, CONF_KERNEL, tm,
                                 y_ref)
        v_ref[0:CONV_HALO, :] = v_ref[tm:tm + CONV_HALO, :]


def _inproj(h, w_in, conf_w, conf_b, layer):
    tm, tn, half = INPROJ_TM, INPROJ_TN, INPROJ_HALF
    row_tiles = ROWS // tm
    conf = lambda j: jnp.clip(j - CONF_FIRST_TILE, 0, CONF_COL_TILES - 1)
    is_conf = lambda j: (j >= CONF_FIRST_TILE) & (j < CONF_FIRST_TILE + CONF_COL_TILES)
    val0, gate0 = OFF_POOL // half, (OFF_POOL + CONF_WIDTH) // half
    lo_block = lambda j: jnp.where(is_conf(j), val0 + conf(j), 2 * j)
    hi_block = lambda j: jnp.where(is_conf(j), gate0 + conf(j), 2 * j + 1)

    def y_index(j, i):
        row = jnp.where(j < CONF_FIRST_TILE, 0,
                        jnp.where(j >= CONF_FIRST_TILE + CONF_COL_TILES, row_tiles - 1, i))
        return (row, conf(j))

    est = (2 * tm * D_MODEL * 2 + 2 * 2 * D_MODEL * half * 4 + D_MODEL * tn * 2 + 2 * tm * tn * 2
           + 2 * tm * half * 4 + (CONV_HALO + tm) * half * 4
           + (SUBLANES - 1) * (CONV_HALO + tm) * LANES * 4 + 4 * tm * 2 * LANES * 4)
    return pl.pallas_call(
        _inproj_kernel,
        grid=(D_IN // tn, row_tiles),
        in_specs=[pl.BlockSpec((tm, D_MODEL), lambda j, i: (i, 0)),
                  pl.BlockSpec((None, D_MODEL, half), lambda j, i: (layer, 0, lo_block(j))),
                  pl.BlockSpec((None, D_MODEL, half), lambda j, i: (layer, 0, hi_block(j))),
                  pl.BlockSpec((None, CONF_KERNEL, half), lambda j, i: (layer, 0, conf(j))),
                  pl.BlockSpec((None, 1, half), lambda j, i: (layer, 0, conf(j)))],
        out_specs=[pl.BlockSpec((tm, tn), lambda j, i: (i, j)),
                   pl.BlockSpec((tm, half), y_index)],
        out_shape=[jax.ShapeDtypeStruct((ROWS, D_IN), BF16),
                   jax.ShapeDtypeStruct((ROWS, CONF_WIDTH), F32)],
        scratch_shapes=[pltpu.VMEM((D_MODEL, tn), BF16),
                        pltpu.VMEM((CONV_HALO + tm, half), F32),
                        pltpu.VMEM((SUBLANES - 1, CONV_HALO + tm - SUBLANES, LANES), F32)],
        compiler_params=_compiler_params(("arbitrary", "arbitrary"), est),
        name="inproj",
    )(h, w_in, w_in, conf_w, _layer_row(conf_b))


TM_TM = 256
TM_HALO = 32
TM_TILES = ROWS // TM_TM
TM_TILES_PER_SEQ = SEQ // TM_TM
MIX_NC = 512
SCONV_COLS = 3 * SCONV_WIDTH


def _tokenmix_kernel(up_ref, upp_ref, us_ref, usp_ref, y_ref, zg_ref, x_ref, pw_ref, ps_ref, lg_ref,
                     lb_ref, sw_ref, gb_ref, wa_ref, wb_ref, wc_ref, wo_ref, g_ref, xo_ref, ho_ref,
                     fa_ref, fb_ref, fc_ref, buf_ref, mix_ref, pool_ref):
    tm, halo = TM_TM, TM_HALO
    t_idx = pl.program_id(0) % TM_TILES_PER_SEQ
    keep = jnp.where(t_idx == 0, 0.0, 1.0).astype(F32)

    buf_ref[0:halo, :] = upp_ref[...].astype(F32) * keep
    buf_ref[halo:, :] = up_ref[...].astype(F32)
    pos = (t_idx * tm + 1 + lax.broadcasted_iota(jnp.int32, (tm, 1), 0)).astype(F32)
    ext = halo + tm
    pool_ref[0:SUBLANES, :] = jnp.zeros((SUBLANES, POOL_GROUP_DIM), F32)
    for g, w in enumerate(POOL_WINDOWS):
        cols = slice(g * POOL_GROUP_DIM, (g + 1) * POOL_GROUP_DIM)
        ws = buf_ref[:, cols]
        x = ws[halo:, :]
        d = 1
        while d < w:
            pool_ref[d:d + ext, :] = ws
            ws = ws + pool_ref[0:ext, :]
            d *= 2
        inv_count = 1.0 / jnp.minimum(pos, float(w))
        pooled = ws[halo:, :] * inv_count - x
        mixed = jnp.dot(pooled.astype(BF16), pw_ref[g].astype(BF16), preferred_element_type=F32)
        fa_ref[:, cols] = (mixed * ps_ref[:, cols]).astype(fa_ref.dtype)

    def sc(ref, part):
        return ref[:, part * SCONV_WIDTH:(part + 1) * SCONV_WIDTH].astype(F32)

    buf_ref[0:halo, :] = sc(usp_ref, 1) * sc(usp_ref, 2) * keep
    buf_ref[halo:, :] = sc(us_ref, 1) * sc(us_ref, 2)
    base = halo - (SCONV_KERNEL - 1)
    q = sw_ref[0:1, :] * buf_ref[base:base + tm, :]
    for k in range(1, SCONV_KERNEL):
        q = q + sw_ref[k:k + 1, :] * buf_ref[base + k:base + k + tm, :]
    fc_ref[...] = (sc(us_ref, 0) * q).astype(fc_ref.dtype)

    y = y_ref[...]
    mu = jnp.mean(y, axis=-1, keepdims=True)
    yc = y - mu
    var = jnp.mean(yc * yc, axis=-1, keepdims=True)
    yn = yc * lax.rsqrt(var + EPS) * lg_ref[...] + lb_ref[...]
    fb_ref[...] = (yn * _sigmoid(yn)).astype(fb_ref.dtype)

    for c in range(D_MODEL // MIX_NC):
        cols = slice(c * MIX_NC, (c + 1) * MIX_NC)
        acc = None
        for k, (f_ref, w_ref) in enumerate(((fa_ref, wa_ref), (fb_ref, wb_ref), (fc_ref, wc_ref))):
            gcols = slice(k * D_MODEL + c * MIX_NC, k * D_MODEL + (c + 1) * MIX_NC)
            gate = _sigmoid(zg_ref[:, gcols].astype(F32) + gb_ref[:, gcols])
            term = gate * jnp.dot(f_ref[...], w_ref[:, cols], preferred_element_type=F32)
            acc = term if acc is None else acc + term
        mix_ref[:, cols] = acc.astype(mix_ref.dtype)

    for c in range(D_MODEL // MIX_NC):
        cols = slice(c * MIX_NC, (c + 1) * MIX_NC)
        xo_ref[:, cols] = x_ref[:, cols] + jnp.dot(mix_ref[...], wo_ref[:, cols],
                                                   preferred_element_type=F32)
    ho_ref[...] = _rms_scale(xo_ref[...], g_ref[...]).astype(ho_ref.dtype)


def _tokenmix(z, y, x, layer, pool_w, pool_scale, ln_g, ln_b, sconv_w, gate_b, wa, wb, wc, wo,
              norm_g):
    tm, halo = TM_TM, TM_HALO
    ratio = tm // halo
    halo_tile = lambda s: jnp.maximum(s * ratio - 1, 0)
    lspec = lambda shape, **kw: _layer_spec(shape, layer, 1, **kw)
    once = dict(pipeline_mode=pl.Buffered(1))
    row_spec = pl.BlockSpec((tm, D_MODEL), lambda s: (s, 0))
    feat = pltpu.VMEM((tm, BRANCH_WIDTH), BF16)
    est = (2 * (tm + halo) * (POOL_WIDTH + SCONV_COLS) * 2 + 2 * tm * CONF_WIDTH * 4
           + 2 * tm * GATE_WIDTH * 2 + 2 * tm * D_MODEL * (4 + 4 + 2)
           + POOL_WIDTH * POOL_GROUP_DIM * 4 + (3 * BRANCH_WIDTH + D_MODEL) * D_MODEL * 2
           + (tm + halo) * BRANCH_WIDTH * 4 + 3 * tm * BRANCH_WIDTH * 2 + tm * D_MODEL * 2
           + 8 * tm * MIX_NC * 4)
    return pl.pallas_call(
        _tokenmix_kernel,
        grid=(TM_TILES,),
        in_specs=[pl.BlockSpec((tm, POOL_WIDTH), lambda s: (s, 0)),
                  pl.BlockSpec((halo, POOL_WIDTH), lambda s: (halo_tile(s), 0)),
                  pl.BlockSpec((tm, SCONV_COLS), lambda s: (s, OFF_CONF // SCONV_COLS)),
                  pl.BlockSpec((halo, SCONV_COLS), lambda s: (halo_tile(s), OFF_CONF // SCONV_COLS)),
                  pl.BlockSpec((tm, CONF_WIDTH), lambda s: (s, 0)),
                  pl.BlockSpec((tm, GATE_WIDTH), lambda s: (s, OFF_SCONV // GATE_WIDTH)),
                  row_spec,
                  lspec((POOL_GROUPS, POOL_GROUP_DIM, POOL_GROUP_DIM), **once),
                  lspec((1, POOL_WIDTH)),
                  lspec((1, CONF_WIDTH)), lspec((1, CONF_WIDTH)),
                  lspec((SCONV_KERNEL, SCONV_WIDTH)),
                  lspec((1, GATE_WIDTH)),
                  lspec((BRANCH_WIDTH, D_MODEL), **once), lspec((BRANCH_WIDTH, D_MODEL), **once),
                  lspec((BRANCH_WIDTH, D_MODEL), **once), lspec((D_MODEL, D_MODEL), **once),
                  lspec((1, D_MODEL))],
        out_specs=[row_spec, row_spec],
        out_shape=[jax.ShapeDtypeStruct((ROWS, D_MODEL), F32),
                   jax.ShapeDtypeStruct((ROWS, D_MODEL), BF16)],
        scratch_shapes=[feat, feat, feat,
                        pltpu.VMEM((halo + tm, BRANCH_WIDTH), F32),
                        pltpu.VMEM((tm, D_MODEL), BF16),
                        pltpu.VMEM((halo + tm + max(POOL_WINDOWS) // 2, POOL_GROUP_DIM), F32)],
        compiler_params=_compiler_params(("arbitrary",), est),
        name="tokenmix",
    )(z, z, z, z, y, z, x, pool_w, _layer_row(pool_scale), _layer_row(ln_g), _layer_row(ln_b),
      sconv_w, _layer_row(gate_b), wa, wb, wc, wo, _layer_row(norm_g))


UP_TM = 1024
UP_TN = 512
UP_ROW_TILES = ROWS // UP_TM
UP_TILES_PER_SEQ = SEQ // UP_TM
UP_COL_TILES = pl.cdiv(D_FF, UP_TN)
D_FF_PAD = UP_COL_TILES * UP_TN
UP_LAST_SHIFT = D_FF_PAD - D_FF
UP_TAIL = SUBLANES
DOWN_SLAB = 256
DOWN_SLABS_PER_COL_TILE = UP_TN // DOWN_SLAB
DOWN_SLAB_EVERY = UP_ROW_TILES // DOWN_SLABS_PER_COL_TILE
DOWN_LAST_SLAB_ROWS = D_FF - (D_FF_PAD - DOWN_SLAB)


def _up_col_start(j, offset=0):
    tile = jnp.minimum(j * (UP_TN // LANES), (D_FF - UP_TN) // LANES)
    return (offset // LANES + tile) * LANES


def _down_slab(j, i):
    return j * DOWN_SLABS_PER_COL_TILE + i // DOWN_SLAB_EVERY


def _ffn_up_kernel(h_ref, wg_ref, wv_ref, cg_ref, cv_ref, wd_ref, o_ref, wdo_ref,
                   wg_bf, wv_bf, cw_ref, ug_ref, uv_ref):
    tm, tn = UP_TM, UP_TN
    j, i = pl.program_id(0), pl.program_id(1)
    last = UP_COL_TILES - 1

    def stage_weights(shift):
        keep = tn - shift
        _cast_weight_tile(wg_ref.at[0], wg_bf, shift)
        _cast_weight_tile(wv_ref.at[0], wv_bf, shift)
        for r, ref in enumerate((cg_ref, cv_ref)):
            rows = slice(r * SUBLANES, r * SUBLANES + FFN_KERNEL)
            cw_ref[rows, 0:keep] = ref[0, :, shift:tn]
            if shift:
                cw_ref[rows, keep:tn] = jnp.zeros((FFN_KERNEL, shift), F32)

    @pl.when((i == 0) & (j != last))
    def _():
        stage_weights(0)

    @pl.when((i == 0) & (j == last))
    def _():
        stage_weights(UP_LAST_SHIFT)

    def stage_down(rows):
        wdo_ref[0:rows, :] = wd_ref[0:rows, :].astype(BF16)
        if rows < DOWN_SLAB:
            wdo_ref[rows:DOWN_SLAB, :] = jnp.zeros((DOWN_SLAB - rows, D_MODEL), BF16)

    is_visit = i % DOWN_SLAB_EVERY == 0
    is_last_slab = (j == last) & (i // DOWN_SLAB_EVERY == DOWN_SLABS_PER_COL_TILE - 1)

    @pl.when(is_visit & jnp.logical_not(is_last_slab))
    def _():
        stage_down(DOWN_SLAB)

    @pl.when(is_visit & is_last_slab)
    def _():
        stage_down(DOWN_LAST_SLAB_ROWS)

    @pl.when(i % UP_TILES_PER_SEQ == 0)
    def _():
        for u_ref in (ug_ref, uv_ref):
            for k in range(1, FFN_KERNEL):
                u_ref[k, tm:tm + k, :] = jnp.zeros((k, tn), F32)

    for u_ref in (ug_ref, uv_ref):
        for k in range(1, FFN_KERNEL):
            u_ref[k, 0:k, :] = u_ref[k, tm:tm + k, :]

    h = h_ref[...]
    for u_ref, w_bf in ((ug_ref, wg_bf), (uv_ref, wv_bf)):
        up = jnp.dot(h, w_bf[...], preferred_element_type=F32)
        for k in range(FFN_KERNEL):
            u_ref[k, k:k + tm, :] = up

    def conv(u_ref, w_row):
        acc = None
        for k in range(FFN_KERNEL):
            src = u_ref[FFN_KERNEL - 1 - k, 0:tm, :]
            term = cw_ref[w_row + k:w_row + k + 1, :] * src
            acc = term if acc is None else acc + term
        return acc

    gt = conv(ug_ref, 0)
    vl = conv(uv_ref, SUBLANES)
    o_ref[...] = (gt * _sigmoid(gt) * vl).astype(o_ref.dtype)


def _ffn_up(h, ffn_up, ffn_conv_w, ffn_down, layer):
    tm, tn = UP_TM, UP_TN
    w_spec = lambda off: pl.BlockSpec((pl.Element(1), pl.Element(D_MODEL), pl.Element(tn)),
                                      lambda j, i: (layer, 0, _up_col_start(j, off)))
    cw_spec = lambda off: pl.BlockSpec((pl.Element(1), pl.Element(FFN_KERNEL), pl.Element(tn)),
                                       lambda j, i: (layer, 0, _up_col_start(j, off)))
    u_scratch = pltpu.VMEM((FFN_KERNEL, tm + UP_TAIL, tn), F32)
    est = (2 * tm * D_MODEL * 2 + 2 * 2 * D_MODEL * tn * 4 + 2 * D_MODEL * tn * 2 + 2 * tm * tn * 2
           + 2 * FFN_KERNEL * (tm + UP_TAIL) * tn * 4 + 2 * DOWN_SLAB * D_MODEL * (4 + 2)
           + 4 * tm * tn * 4)
    return pl.pallas_call(
        _ffn_up_kernel,
        grid=(UP_COL_TILES, UP_ROW_TILES),
        in_specs=[pl.BlockSpec((tm, D_MODEL), lambda j, i: (i, 0)),
                  w_spec(0), w_spec(D_FF), cw_spec(0), cw_spec(D_FF),
                  pl.BlockSpec((None, DOWN_SLAB, D_MODEL),
                               lambda j, i: (layer, _down_slab(j, i), 0))],
        out_specs=[pl.BlockSpec((tm, tn), lambda j, i: (i, j)),
                   pl.BlockSpec((DOWN_SLAB, D_MODEL), lambda j, i: (_down_slab(j, i), 0))],
        out_shape=[jax.ShapeDtypeStruct((ROWS, D_FF_PAD), BF16),
                   jax.ShapeDtypeStruct((D_FF_PAD, D_MODEL), BF16)],
        scratch_shapes=[pltpu.VMEM((D_MODEL, tn), BF16), pltpu.VMEM((D_MODEL, tn), BF16),
                        pltpu.VMEM((2 * SUBLANES, tn), F32), u_scratch, u_scratch],
        compiler_params=_compiler_params(("arbitrary", "arbitrary"), est),
        name="ffn_up",
    )(h, ffn_up, ffn_up, ffn_conv_w, ffn_conv_w, ffn_down)


DOWN_TM = 512


def _ffn_down_kernel(act_ref, w_ref, x_ref, g_ref, *out_refs, last):
    xn = x_ref[...] + jnp.dot(act_ref[...], w_ref[...], preferred_element_type=F32)
    normed = _rms_scale(xn, g_ref[...])
    if last:
        (y_ref,) = out_refs
        y_ref[...] = normed
    else:
        xo_ref, ho_ref = out_refs
        xo_ref[...] = xn
        ho_ref[...] = normed.astype(ho_ref.dtype)


def _ffn_down(act, wd, x, g, last):
    tm = DOWN_TM
    row_spec = pl.BlockSpec((tm, D_MODEL), lambda i: (i, 0))
    if last:
        out_specs = [row_spec]
        out_shape = [jax.ShapeDtypeStruct((ROWS, D_MODEL), F32)]
    else:
        out_specs = [row_spec, row_spec]
        out_shape = [jax.ShapeDtypeStruct((ROWS, D_MODEL), F32),
                     jax.ShapeDtypeStruct((ROWS, D_MODEL), BF16)]
    est = (2 * tm * D_FF_PAD * 2 + D_FF_PAD * D_MODEL * 2 + 2 * tm * D_MODEL * (4 + 4 + 2)
           + 3 * tm * D_MODEL * 4)
    return pl.pallas_call(
        functools.partial(_ffn_down_kernel, last=last),
        grid=(ROWS // tm,),
        in_specs=[pl.BlockSpec((tm, D_FF_PAD), lambda i: (i, 0)),
                  pl.BlockSpec((D_FF_PAD, D_MODEL), lambda i: (0, 0), pipeline_mode=pl.Buffered(1)),
                  row_spec,
                  pl.BlockSpec((1, D_MODEL), lambda i: (0, 0))],
        out_specs=out_specs,
        out_shape=out_shape,
        compiler_params=_compiler_params(("arbitrary",), est),
        name="ffn_down",
    )(act, wd, x, g.reshape(1, -1))


def kernel(x, norm1_g, w_in, gate_b, pool_w, pool_scale, pool_proj, conf_conv_w, conf_conv_b,
           conf_ln_g, conf_ln_b, conf_proj, sconv_w, sconv_proj, w_o, norm2_g, ffn_up, ffn_conv_w,
           ffn_down, final_g):
    assert x.shape == (BATCH, SEQ, D_MODEL) and w_in.shape == (DEPTH, D_MODEL, D_IN)
    xs = x.reshape(ROWS, D_MODEL)
    wa, wb, wc, wo = (w.astype(BF16) for w in (pool_proj, conf_proj, sconv_proj, w_o))
    h = _rmsnorm_cast(xs, norm1_g, 0)
    for l in range(DEPTH):
        z, conv_y = _inproj(h, w_in, conf_conv_w, conf_conv_b, l)
        xs, h = _tokenmix(z, conv_y, xs, l, pool_w, pool_scale, conf_ln_g, conf_ln_b, sconv_w, gate_b,
                          wa, wb, wc, wo, norm2_g)
        act, wd = _ffn_up(h, ffn_up, ffn_conv_w, ffn_down, l)
        last = l == DEPTH - 1
        g_next = final_g if last else norm1_g[l + 1]
        outs = _ffn_down(act, wd, xs, g_next, last)
        if last:
            (y,) = outs
        else:
            xs, h = outs
    return y.reshape(BATCH, SEQ, D_MODEL)
```

```python
import functools

import jax
import jax.numpy as jnp
from jax import lax
from jax.experimental import pallas as pl
from jax.experimental.pallas import tpu as pltpu

D_MODEL = 2048
BATCH = 4
SEQ = 2048
DEPTH = 2
ROWS = BATCH * SEQ

POOL_WIDTH = 1024
POOL_GROUPS = 4
POOL_WINDOWS = (2, 4, 8, 16)
POOL_GROUP_DIM = POOL_WIDTH // POOL_GROUPS
CONF_WIDTH = 1024
CONF_KERNEL = 31
SCONV_WIDTH = 1024
SCONV_KERNEL = 3
N_BRANCH = 3
BRANCH_WIDTH = 1024
OFF_POOL = POOL_WIDTH
OFF_CONF = OFF_POOL + 2 * CONF_WIDTH
OFF_SCONV = OFF_CONF + 3 * SCONV_WIDTH
GATE_WIDTH = N_BRANCH * D_MODEL
D_IN = OFF_SCONV + GATE_WIDTH
D_FF = 5504
FFN_KERNEL = 3
EPS = 1e-6

V7X_VMEM_BYTES = 64 * 1024 * 1024
SUBLANES = 8
LANES = 128

BF16 = jnp.bfloat16
F32 = jnp.float32


def _compiler_params(semantics, vmem_estimate_bytes):
    limit = min(int(vmem_estimate_bytes * 1.25) + (4 << 20), V7X_VMEM_BYTES - (4 << 20))
    return pltpu.CompilerParams(dimension_semantics=semantics, vmem_limit_bytes=limit)


def _sigmoid(x):
    return 1.0 / (1.0 + jnp.exp(-x))


def _rms_scale(x, g):
    ms = jnp.mean(x * x, axis=-1, keepdims=True)
    return x * lax.rsqrt(ms + EPS) * g


def _layer_row(stacked):
    return stacked.reshape(stacked.shape[0], 1, stacked.shape[1])


def _layer_spec(shape, layer, grid_rank, **kwargs):
    zeros = (0,) * len(shape)
    if grid_rank == 1:
        index_map = lambda i: (layer,) + zeros
    else:
        index_map = lambda j, i: (layer,) + zeros
    return pl.BlockSpec((None,) + tuple(shape), index_map, **kwargs)


NORM_TM = 512


def _rmsnorm_kernel(x_ref, g_ref, o_ref):
    o_ref[...] = _rms_scale(x_ref[...], g_ref[...]).astype(o_ref.dtype)


def _rmsnorm_cast(x, g, layer):
    tm = NORM_TM
    est = 2 * tm * D_MODEL * (4 + 2) + 4 * tm * D_MODEL * 4
    return pl.pallas_call(
        _rmsnorm_kernel,
        grid=(ROWS // tm,),
        in_specs=[pl.BlockSpec((tm, D_MODEL), lambda i: (i, 0)),
                  _layer_spec((1, D_MODEL), layer, 1)],
        out_specs=pl.BlockSpec((tm, D_MODEL), lambda i: (i, 0)),
        out_shape=jax.ShapeDtypeStruct((ROWS, D_MODEL), BF16),
        compiler_params=_compiler_params(("arbitrary",), est),
        name="rmsnorm_cast",
    )(x, _layer_row(g))


CONV_HALO = 32
CONV_ROW_BLOCK = 256


def _shifted_causal_conv(buf_ref, shf_ref, cw_ref, bias, cols, taps, tm, out_ref):
    base = CONV_HALO - (taps - 1)
    span = CONV_HALO + tm - SUBLANES
    residues = sorted({(base + k) % SUBLANES for k in range(taps)} - {0})
    for r in residues:
        shf_ref[r - 1, :, :] = buf_ref[r:r + span, cols]
    for lo in range(0, tm, CONV_ROW_BLOCK):
        acc = bias
        for k in range(taps):
            a, r = divmod(base + k, SUBLANES)
            row = a * SUBLANES + lo
            if r == 0:
                src = buf_ref[row:row + CONV_ROW_BLOCK, cols]
            else:
                src = shf_ref[r - 1, row:row + CONV_ROW_BLOCK, :]
            term = cw_ref[k:k + 1, cols] * src
            acc = term if acc is None else acc + term
        out_ref[lo:lo + CONV_ROW_BLOCK, cols] = acc.astype(out_ref.dtype)


INPROJ_TM = 1024
INPROJ_TN = 1024
INPROJ_HALF = INPROJ_TN // 2
INPROJ_ROW_TILES = ROWS // INPROJ_TM
INPROJ_TILES_PER_SEQ = SEQ // INPROJ_TM
PAIR_TILES = CONF_WIDTH // INPROJ_HALF
TILE_POOL = 0
TILE_CONF = TILE_POOL + POOL_WIDTH // INPROJ_TN
TILE_BG = TILE_CONF + PAIR_TILES
TILE_SC = TILE_BG + SCONV_WIDTH // INPROJ_TN
TILE_GATE = TILE_SC + PAIR_TILES
GATE_TILES = GATE_WIDTH // INPROJ_TN
INPROJ_COL_TILES = TILE_GATE + GATE_TILES
Z_POOL_BLOCK = GATE_TILES
Z_BG_BLOCK = Z_POOL_BLOCK + POOL_WIDTH // INPROJ_TN
Z_WIDTH = GATE_WIDTH + POOL_WIDTH + SCONV_WIDTH
CAST_ROWS = 256


def _cast_weight_tile(w_ref, wbf_ref, lane_shift=0):
    rows, cols = wbf_ref.shape
    keep = cols - lane_shift

    def body(r, carry):
        sl = pl.ds(pl.multiple_of(r * CAST_ROWS, CAST_ROWS), CAST_ROWS)
        wbf_ref[sl, 0:keep] = w_ref[sl, lane_shift:cols].astype(BF16)
        if lane_shift:
            wbf_ref[sl, keep:cols] = jnp.zeros((CAST_ROWS, lane_shift), BF16)
        return carry

    lax.fori_loop(0, rows // CAST_ROWS, body, 0)


def _cast_weight_pairs(wlo_ref, whi_ref, wbf_ref):
    rows = wbf_ref.shape[0]

    def body(r, carry):
        sl = pl.ds(pl.multiple_of(r * CAST_ROWS, CAST_ROWS), CAST_ROWS)
        for c in range(INPROJ_HALF // LANES):
            src = slice(c * LANES, (c + 1) * LANES)
            wbf_ref[sl, 2 * c * LANES:(2 * c + 1) * LANES] = wlo_ref[sl, src].astype(BF16)
            wbf_ref[sl, (2 * c + 1) * LANES:(2 * c + 2) * LANES] = whi_ref[sl, src].astype(BF16)
        return carry

    lax.fori_loop(0, rows // CAST_ROWS, body, 0)


def _inproj_kernel(h_ref, wlo_ref, whi_ref, cw_ref, cb_ref, sw_ref, z_ref, y_ref, q_ref,
                   wbf_ref, v_ref, shf_ref):
    tm, half = INPROJ_TM, INPROJ_HALF
    j, i = pl.program_id(0), pl.program_id(1)
    is_conf = (j >= TILE_CONF) & (j < TILE_CONF + PAIR_TILES)
    is_sc = (j >= TILE_SC) & (j < TILE_SC + PAIR_TILES)

    @pl.when(jnp.logical_not(is_conf | is_sc))
    def _():
        @pl.when(i == 0)
        def _():
            _cast_weight_tile(wlo_ref, wbf_ref.at[:, 0:half])
            _cast_weight_tile(whi_ref, wbf_ref.at[:, half:2 * half])

        z_ref[...] = jnp.dot(h_ref[...], wbf_ref[...],
                             preferred_element_type=F32).astype(z_ref.dtype)

    def paired_tile(combine, conv_w_ref, bias_ref, taps, out_ref):
        @pl.when(i == 0)
        def _():
            _cast_weight_pairs(wlo_ref, whi_ref, wbf_ref)

        @pl.when(i % INPROJ_TILES_PER_SEQ == 0)
        def _():
            v_ref[0:CONV_HALO, :] = jnp.zeros((CONV_HALO, half), F32)

        h = h_ref[...]
        for c in range(half // LANES):
            cols = slice(c * LANES, (c + 1) * LANES)
            pair = jnp.dot(h, wbf_ref[:, 2 * c * LANES:(2 * c + 2) * LANES],
                           preferred_element_type=F32)
            v_ref[CONV_HALO:, cols] = combine(pair[:, 0:LANES], pair[:, LANES:2 * LANES])
            bias = None if bias_ref is None else bias_ref[:, cols]
            _shifted_causal_conv(v_ref, shf_ref, conv_w_ref, bias, cols, taps, tm, out_ref)
        v_ref[0:CONV_HALO, :] = v_ref[tm:tm + CONV_HALO, :]

    @pl.when(is_conf)
    def _():
        paired_tile(lambda val, gte: val * _sigmoid(gte), cw_ref, cb_ref, CONF_KERNEL, y_ref)

    @pl.when(is_sc)
    def _():
        paired_tile(lambda cg, hs: cg * hs, sw_ref, None, SCONV_KERNEL, q_ref)


def _inproj(h, w_in, conf_w, conf_b, sconv_w, layer):
    tm, tn, half = INPROJ_TM, INPROJ_TN, INPROJ_HALF
    last_row = INPROJ_ROW_TILES - 1
    clip = lambda j, first: jnp.clip(j - first, 0, PAIR_TILES - 1)
    is_conf = lambda j: (j >= TILE_CONF) & (j < TILE_CONF + PAIR_TILES)
    is_sc = lambda j: (j >= TILE_SC) & (j < TILE_SC + PAIR_TILES)
    val_hb, gte_hb = OFF_POOL // half, (OFF_POOL + CONF_WIDTH) // half
    bg_hb = OFF_CONF // half
    cg_hb, hs_hb = (OFF_CONF + SCONV_WIDTH) // half, (OFF_CONF + 2 * SCONV_WIDTH) // half
    gate_hb = OFF_SCONV // half

    def plain_hb(j):
        return jnp.where(j < TILE_CONF, 2 * (j - TILE_POOL),
                         jnp.where(j < TILE_SC, bg_hb + 2 * (j - TILE_BG),
                                   gate_hb + 2 * (j - TILE_GATE)))

    lo_block = lambda j: jnp.where(is_conf(j), val_hb + clip(j, TILE_CONF),
                                   jnp.where(is_sc(j), cg_hb + clip(j, TILE_SC), plain_hb(j)))
    hi_block = lambda j: jnp.where(is_conf(j), gte_hb + clip(j, TILE_CONF),
                                   jnp.where(is_sc(j), hs_hb + clip(j, TILE_SC), plain_hb(j) + 1))

    def z_index(j, i):
        held = is_conf(j) | is_sc(j)
        row = jnp.where(held, last_row, i)
        col = jnp.where(j < TILE_BG, Z_POOL_BLOCK,
                        jnp.where(j < TILE_GATE, Z_BG_BLOCK, j - TILE_GATE))
        return (row, col)

    def pair_index(first):
        def index(j, i):
            row = jnp.where(j < first, 0, jnp.where(j >= first + PAIR_TILES, last_row, i))
            return (row, clip(j, first))
        return index

    est = (2 * tm * D_MODEL * 2 + 2 * 2 * D_MODEL * half * 4 + D_MODEL * tn * 2 + 2 * tm * tn * 2
           + 2 * tm * half * (4 + 2) + (CONV_HALO + tm) * half * 4
           + (SUBLANES - 1) * (CONV_HALO + tm) * LANES * 4 + 4 * tm * 2 * LANES * 4)
    return pl.pallas_call(
        _inproj_kernel,
        grid=(INPROJ_COL_TILES, INPROJ_ROW_TILES),
        in_specs=[pl.BlockSpec((tm, D_MODEL), lambda j, i: (i, 0)),
                  pl.BlockSpec((None, D_MODEL, half), lambda j, i: (layer, 0, lo_block(j))),
                  pl.BlockSpec((None, D_MODEL, half), lambda j, i: (layer, 0, hi_block(j))),
                  pl.BlockSpec((None, CONF_KERNEL, half),
                               lambda j, i: (layer, 0, clip(j, TILE_CONF))),
                  pl.BlockSpec((None, 1, half), lambda j, i: (layer, 0, clip(j, TILE_CONF))),
                  pl.BlockSpec((None, SCONV_KERNEL, half),
                               lambda j, i: (layer, 0, clip(j, TILE_SC)))],
        out_specs=[pl.BlockSpec((tm, tn), z_index),
                   pl.BlockSpec((tm, half), pair_index(TILE_CONF)),
                   pl.BlockSpec((tm, half), pair_index(TILE_SC))],
        out_shape=[jax.ShapeDtypeStruct((ROWS, Z_WIDTH), BF16),
                   jax.ShapeDtypeStruct((ROWS, CONF_WIDTH), F32),
                   jax.ShapeDtypeStruct((ROWS, SCONV_WIDTH), BF16)],
        scratch_shapes=[pltpu.VMEM((D_MODEL, tn), BF16),
                        pltpu.VMEM((CONV_HALO + tm, half), F32),
                        pltpu.VMEM((SUBLANES - 1, CONV_HALO + tm - SUBLANES, LANES), F32)],
        compiler_params=_compiler_params(("arbitrary", "arbitrary"), est),
        name="inproj",
    )(h, w_in, w_in, conf_w, _layer_row(conf_b), sconv_w)


TM_TM = 256
TM_HALO = 32
TM_TILES = ROWS // TM_TM
TM_TILES_PER_SEQ = SEQ // TM_TM
MIX_NC = 512


def _tokenmix_kernel(up_ref, upp_ref, bg_ref, q_ref, y_ref, zg_ref, x_ref, pw_ref, ps_ref, lg_ref,
                     lb_ref, gb_ref, wa_ref, wb_ref, wc_ref, wo_ref, g_ref, xo_ref, ho_ref,
                     fa_ref, fb_ref, fc_ref, buf_ref, mix_ref, pool_ref):
    tm, halo = TM_TM, TM_HALO
    t_idx = pl.program_id(0) % TM_TILES_PER_SEQ
    keep = jnp.where(t_idx == 0, 0.0, 1.0).astype(F32)

    buf_ref[0:halo, :] = upp_ref[...].astype(F32) * keep
    buf_ref[halo:, :] = up_ref[...].astype(F32)
    pos = (t_idx * tm + 1 + lax.broadcasted_iota(jnp.int32, (tm, 1), 0)).astype(F32)
    ext = halo + tm
    pool_ref[0:SUBLANES, :] = jnp.zeros((SUBLANES, POOL_GROUP_DIM), F32)
    for g, w in enumerate(POOL_WINDOWS):
        cols = slice(g * POOL_GROUP_DIM, (g + 1) * POOL_GROUP_DIM)
        ws = buf_ref[:, cols]
        x = ws[halo:, :]
        d = 1
        while d < w:
            pool_ref[d:d + ext, :] = ws
            ws = ws + pool_ref[0:ext, :]
            d *= 2
        inv_count = 1.0 / jnp.minimum(pos, float(w))
        pooled = ws[halo:, :] * inv_count - x
        mixed = jnp.dot(pooled.astype(BF16), pw_ref[g].astype(BF16), preferred_element_type=F32)
        fa_ref[:, cols] = (mixed * ps_ref[:, cols]).astype(fa_ref.dtype)

    fc_ref[...] = (bg_ref[...].astype(F32) * q_ref[...].astype(F32)).astype(fc_ref.dtype)

    y = y_ref[...]
    mu = jnp.mean(y, axis=-1, keepdims=True)
    yc = y - mu
    var = jnp.mean(yc * yc, axis=-1, keepdims=True)
    yn = yc * lax.rsqrt(var + EPS) * lg_ref[...] + lb_ref[...]
    fb_ref[...] = (yn * _sigmoid(yn)).astype(fb_ref.dtype)

    for c in range(D_MODEL // MIX_NC):
        cols = slice(c * MIX_NC, (c + 1) * MIX_NC)
        acc = None
        for k, (f_ref, w_ref) in enumerate(((fa_ref, wa_ref), (fb_ref, wb_ref), (fc_ref, wc_ref))):
            gcols = slice(k * D_MODEL + c * MIX_NC, k * D_MODEL + (c + 1) * MIX_NC)
            gate = _sigmoid(zg_ref[:, gcols].astype(F32) + gb_ref[:, gcols])
            term = gate * jnp.dot(f_ref[...], w_ref[:, cols], preferred_element_type=F32)
            acc = term if acc is None else acc + term
        mix_ref[:, cols] = acc.astype(mix_ref.dtype)

    for c in range(D_MODEL // MIX_NC):
        cols = slice(c * MIX_NC, (c + 1) * MIX_NC)
        xo_ref[:, cols] = x_ref[:, cols] + jnp.dot(mix_ref[...], wo_ref[:, cols],
                                                   preferred_element_type=F32)
    ho_ref[...] = _rms_scale(xo_ref[...], g_ref[...]).astype(ho_ref.dtype)


def _tokenmix(z, y, q, x, layer, pool_w, pool_scale, ln_g, ln_b, gate_b, wa, wb, wc, wo, norm_g):
    tm, halo = TM_TM, TM_HALO
    ratio = tm // halo
    halo_tile = lambda s: jnp.maximum(s * ratio - 1, 0)
    lspec = lambda shape, **kw: _layer_spec(shape, layer, 1, **kw)
    once = dict(pipeline_mode=pl.Buffered(1))
    row_spec = pl.BlockSpec((tm, D_MODEL), lambda s: (s, 0))
    feat = pltpu.VMEM((tm, BRANCH_WIDTH), BF16)
    est = (2 * (tm + halo) * POOL_WIDTH * 2 + 2 * 2 * tm * SCONV_WIDTH * 2 + 2 * tm * CONF_WIDTH * 4
           + 2 * tm * GATE_WIDTH * 2 + 2 * tm * D_MODEL * (4 + 4 + 2)
           + POOL_WIDTH * POOL_GROUP_DIM * 4 + (3 * BRANCH_WIDTH + D_MODEL) * D_MODEL * 2
           + (tm + halo) * BRANCH_WIDTH * 4 + 3 * tm * BRANCH_WIDTH * 2 + tm * D_MODEL * 2
           + 8 * tm * MIX_NC * 4)
    return pl.pallas_call(
        _tokenmix_kernel,
        grid=(TM_TILES,),
        in_specs=[pl.BlockSpec((tm, POOL_WIDTH), lambda s: (s, Z_POOL_BLOCK)),
                  pl.BlockSpec((halo, POOL_WIDTH), lambda s: (halo_tile(s), Z_POOL_BLOCK)),
                  pl.BlockSpec((tm, SCONV_WIDTH), lambda s: (s, Z_BG_BLOCK)),
                  pl.BlockSpec((tm, SCONV_WIDTH), lambda s: (s, 0)),
                  pl.BlockSpec((tm, CONF_WIDTH), lambda s: (s, 0)),
                  pl.BlockSpec((tm, GATE_WIDTH), lambda s: (s, 0)),
                  row_spec,
                  lspec((POOL_GROUPS, POOL_GROUP_DIM, POOL_GROUP_DIM), **once),
                  lspec((1, POOL_WIDTH)),
                  lspec((1, CONF_WIDTH)), lspec((1, CONF_WIDTH)),
                  lspec((1, GATE_WIDTH)),
                  lspec((BRANCH_WIDTH, D_MODEL), **once), lspec((BRANCH_WIDTH, D_MODEL), **once),
                  lspec((BRANCH_WIDTH, D_MODEL), **once), lspec((D_MODEL, D_MODEL), **once),
                  lspec((1, D_MODEL))],
        out_specs=[row_spec, row_spec],
        out_shape=[jax.ShapeDtypeStruct((ROWS, D_MODEL), F32),
                   jax.ShapeDtypeStruct((ROWS, D_MODEL), BF16)],
        scratch_shapes=[feat, feat, feat,
                        pltpu.VMEM((halo + tm, BRANCH_WIDTH), F32),
                        pltpu.VMEM((tm, D_MODEL), BF16),
                        pltpu.VMEM((halo + tm + max(POOL_WINDOWS) // 2, POOL_GROUP_DIM), F32)],
        compiler_params=_compiler_params(("arbitrary",), est),
        name="tokenmix",
    )(z, z, z, q, y, z, x, pool_w, _layer_row(pool_scale), _layer_row(ln_g), _layer_row(ln_b),
      _layer_row(gate_b), wa, wb, wc, wo, _layer_row(norm_g))


UP_TM = 1024
UP_TN = 512
UP_ROW_TILES = ROWS // UP_TM
UP_TILES_PER_SEQ = SEQ // UP_TM
UP_COL_TILES = pl.cdiv(D_FF, UP_TN)
D_FF_PAD = UP_COL_TILES * UP_TN
UP_LAST_SHIFT = D_FF_PAD - D_FF
UP_TAIL = SUBLANES
DOWN_SLAB = 256
DOWN_SLABS_PER_COL_TILE = UP_TN // DOWN_SLAB
DOWN_SLAB_EVERY = UP_ROW_TILES // DOWN_SLABS_PER_COL_TILE
DOWN_LAST_SLAB_ROWS = D_FF - (D_FF_PAD - DOWN_SLAB)


def _up_col_start(j, offset=0):
    tile = jnp.minimum(j * (UP_TN // LANES), (D_FF - UP_TN) // LANES)
    return (offset // LANES + tile) * LANES


def _down_slab(j, i):
    return j * DOWN_SLABS_PER_COL_TILE + i // DOWN_SLAB_EVERY


def _ffn_up_kernel(h_ref, wg_ref, wv_ref, cg_ref, cv_ref, wd_ref, o_ref, wdo_ref,
                   wg_bf, wv_bf, cw_ref, ug_ref, uv_ref):
    tm, tn = UP_TM, UP_TN
    j, i = pl.program_id(0), pl.program_id(1)
    last = UP_COL_TILES - 1

    def stage_weights(shift):
        keep = tn - shift
        _cast_weight_tile(wg_ref.at[0], wg_bf, shift)
        _cast_weight_tile(wv_ref.at[0], wv_bf, shift)
        for r, ref in enumerate((cg_ref, cv_ref)):
            rows = slice(r * SUBLANES, r * SUBLANES + FFN_KERNEL)
            cw_ref[rows, 0:keep] = ref[0, :, shift:tn]
            if shift:
                cw_ref[rows, keep:tn] = jnp.zeros((FFN_KERNEL, shift), F32)

    @pl.when((i == 0) & (j != last))
    def _():
        stage_weights(0)

    @pl.when((i == 0) & (j == last))
    def _():
        stage_weights(UP_LAST_SHIFT)

    def stage_down(rows):
        wdo_ref[0:rows, :] = wd_ref[0:rows, :].astype(BF16)
        if rows < DOWN_SLAB:
            wdo_ref[rows:DOWN_SLAB, :] = jnp.zeros((DOWN_SLAB - rows, D_MODEL), BF16)

    is_visit = i % DOWN_SLAB_EVERY == 0
    is_last_slab = (j == last) & (i // DOWN_SLAB_EVERY == DOWN_SLABS_PER_COL_TILE - 1)

    @pl.when(is_visit & jnp.logical_not(is_last_slab))
    def _():
        stage_down(DOWN_SLAB)

    @pl.when(is_visit & is_last_slab)
    def _():
        stage_down(DOWN_LAST_SLAB_ROWS)

    @pl.when(i % UP_TILES_PER_SEQ == 0)
    def _():
        for u_ref in (ug_ref, uv_ref):
            for k in range(1, FFN_KERNEL):
                u_ref[k, tm:tm + k, :] = jnp.zeros((k, tn), F32)

    for u_ref in (ug_ref, uv_ref):
        for k in range(1, FFN_KERNEL):
            u_ref[k, 0:k, :] = u_ref[k, tm:tm + k, :]

    h = h_ref[...]
    for u_ref, w_bf in ((ug_ref, wg_bf), (uv_ref, wv_bf)):
        up = jnp.dot(h, w_bf[...], preferred_element_type=F32)
        for k in range(FFN_KERNEL):
            u_ref[k, k:k + tm, :] = up

    def conv(u_ref, w_row):
        acc = None
        for k in range(FFN_KERNEL):
            src = u_ref[FFN_KERNEL - 1 - k, 0:tm, :]
            term = cw_ref[w_row + k:w_row + k + 1, :] * src
            acc = term if acc is None else acc + term
        return acc

    gt = conv(ug_ref, 0)
    vl = conv(uv_ref, SUBLANES)
    o_ref[...] = (gt * _sigmoid(gt) * vl).astype(o_ref.dtype)


def _ffn_up(h, ffn_up, ffn_conv_w, ffn_down, layer):
    tm, tn = UP_TM, UP_TN
    w_spec = lambda off: pl.BlockSpec((pl.Element(1), pl.Element(D_MODEL), pl.Element(tn)),
                                      lambda j, i: (layer, 0, _up_col_start(j, off)))
    cw_spec = lambda off: pl.BlockSpec((pl.Element(1), pl.Element(FFN_KERNEL), pl.Element(tn)),
                                       lambda j, i: (layer, 0, _up_col_start(j, off)))
    u_scratch = pltpu.VMEM((FFN_KERNEL, tm + UP_TAIL, tn), F32)
    est = (2 * tm * D_MODEL * 2 + 2 * 2 * D_MODEL * tn * 4 + 2 * D_MODEL * tn * 2 + 2 * tm * tn * 2
           + 2 * FFN_KERNEL * (tm + UP_TAIL) * tn * 4 + 2 * DOWN_SLAB * D_MODEL * (4 + 2)
           + 4 * tm * tn * 4)
    return pl.pallas_call(
        _ffn_up_kernel,
        grid=(UP_COL_TILES, UP_ROW_TILES),
        in_specs=[pl.BlockSpec((tm, D_MODEL), lambda j, i: (i, 0)),
                  w_spec(0), w_spec(D_FF), cw_spec(0), cw_spec(D_FF),
                  pl.BlockSpec((None, DOWN_SLAB, D_MODEL),
                               lambda j, i: (layer, _down_slab(j, i), 0))],
        out_specs=[pl.BlockSpec((tm, tn), lambda j, i: (i, j)),
                   pl.BlockSpec((DOWN_SLAB, D_MODEL), lambda j, i: (_down_slab(j, i), 0))],
        out_shape=[jax.ShapeDtypeStruct((ROWS, D_FF_PAD), BF16),
                   jax.ShapeDtypeStruct((D_FF_PAD, D_MODEL), BF16)],
        scratch_shapes=[pltpu.VMEM((D_MODEL, tn), BF16), pltpu.VMEM((D_MODEL, tn), BF16),
                        pltpu.VMEM((2 * SUBLANES, tn), F32), u_scratch, u_scratch],
        compiler_params=_compiler_params(("arbitrary", "arbitrary"), est),
        name="ffn_up",
    )(h, ffn_up, ffn_up, ffn_conv_w, ffn_conv_w, ffn_down)


DOWN_TM = 512


def _ffn_down_kernel(act_ref, w_ref, x_ref, g_ref, *out_refs, last):
    xn = x_ref[...] + jnp.dot(act_ref[...], w_ref[...], preferred_element_type=F32)
    normed = _rms_scale(xn, g_ref[...])
    if last:
        (y_ref,) = out_refs
        y_ref[...] = normed
    else:
        xo_ref, ho_ref = out_refs
        xo_ref[...] = xn
        ho_ref[...] = normed.astype(ho_ref.dtype)


def _ffn_down(act, wd, x, g, last):
    tm = DOWN_TM
    row_spec = pl.BlockSpec((tm, D_MODEL), lambda i: (i, 0))
    if last:
        out_specs = [row_spec]
        out_shape = [jax.ShapeDtypeStruct((ROWS, D_MODEL), F32)]
    else:
        out_specs = [row_spec, row_spec]
        out_shape = [jax.ShapeDtypeStruct((ROWS, D_MODEL), F32),
                     jax.ShapeDtypeStruct((ROWS, D_MODEL), BF16)]
    est = (2 * tm * D_FF_PAD * 2 + D_FF_PAD * D_MODEL * 2 + 2 * tm * D_MODEL * (4 + 4 + 2)
           + 3 * tm * D_MODEL * 4)
    return pl.pallas_call(
        functools.partial(_ffn_down_kernel, last=last),
        grid=(ROWS // tm,),
        in_specs=[pl.BlockSpec((tm, D_FF_PAD), lambda i: (i, 0)),
                  pl.BlockSpec((D_FF_PAD, D_MODEL), lambda i: (0, 0), pipeline_mode=pl.Buffered(1)),
                  row_spec,
                  pl.BlockSpec((1, D_MODEL), lambda i: (0, 0))],
        out_specs=out_specs,
        out_shape=out_shape,
        compiler_params=_compiler_params(("arbitrary",), est),
        name="ffn_down",
    )(act, wd, x, g.reshape(1, -1))


def kernel(x, norm1_g, w_in, gate_b, pool_w, pool_scale, pool_proj, conf_conv_w, conf_conv_b,
           conf_ln_g, conf_ln_b, conf_proj, sconv_w, sconv_proj, w_o, norm2_g, ffn_up, ffn_conv_w,
           ffn_down, final_g):
    assert x.shape == (BATCH, SEQ, D_MODEL) and w_in.shape == (DEPTH, D_MODEL, D_IN)
    xs = x.reshape(ROWS, D_MODEL)
    wa, wb, wc, wo = (w.astype(BF16) for w in (pool_proj, conf_proj, sconv_proj, w_o))
    h = _rmsnorm_cast(xs, norm1_g, 0)
    for l in range(DEPTH):
        z, conv_y, conv_q = _inproj(h, w_in, conf_conv_w, conf_conv_b, sconv_w, l)
        xs, h = _tokenmix(z, conv_y, conv_q, xs, l, pool_w, pool_scale, conf_ln_g, conf_ln_b, gate_b,
                          wa, wb, wc, wo, norm2_g)
        act, wd = _ffn_up(h, ffn_up, ffn_conv_w, ffn_down, l)
        last = l == DEPTH - 1
        g_next = final_g if last else norm1_g[l + 1]
        outs = _ffn_down(act, wd, xs, g_next, last)
        if last:
            (y,) = outs
        else:
            xs, h = outs
    return y.reshape(BATCH, SEQ, D_MODEL)
```

```python
import functools

import jax
import jax.numpy as jnp
from jax import lax
from jax.experimental import pallas as pl
from jax.experimental.pallas import tpu as pltpu

D_MODEL = 2048
BATCH = 4
SEQ = 2048
DEPTH = 2
ROWS = BATCH * SEQ

POOL_WIDTH = 1024
POOL_GROUPS = 4
POOL_WINDOWS = (2, 4, 8, 16)
POOL_GROUP_DIM = POOL_WIDTH // POOL_GROUPS
CONF_WIDTH = 1024
CONF_KERNEL = 31
SCONV_WIDTH = 1024
SCONV_KERNEL = 3
N_BRANCH = 3
BRANCH_WIDTH = 1024
OFF_POOL = POOL_WIDTH
OFF_CONF = OFF_POOL + 2 * CONF_WIDTH
OFF_SCONV = OFF_CONF + 3 * SCONV_WIDTH
GATE_WIDTH = N_BRANCH * D_MODEL
D_IN = OFF_SCONV + GATE_WIDTH
D_FF = 5504
FFN_KERNEL = 3
EPS = 1e-6

V7X_VMEM_BYTES = 64 * 1024 * 1024
SUBLANES = 8
LANES = 128

BF16 = jnp.bfloat16
F32 = jnp.float32


def _compiler_params(semantics, vmem_estimate_bytes):
    limit = min(int(vmem_estimate_bytes * 1.25) + (4 << 20), V7X_VMEM_BYTES - (4 << 20))
    return pltpu.CompilerParams(dimension_semantics=semantics, vmem_limit_bytes=limit)


def _sigmoid(x):
    return 1.0 / (1.0 + jnp.exp(-x))


def _rms_scale(x, g):
    ms = jnp.mean(x * x, axis=-1, keepdims=True)
    return x * lax.rsqrt(ms + EPS) * g


def _layer_row(stacked):
    return stacked.reshape(stacked.shape[0], 1, stacked.shape[1])


def _layer_spec(shape, layer, grid_rank, **kwargs):
    zeros = (0,) * len(shape)
    if grid_rank == 1:
        index_map = lambda i: (layer,) + zeros
    else:
        index_map = lambda j, i: (layer,) + zeros
    return pl.BlockSpec((None,) + tuple(shape), index_map, **kwargs)


NORM_TM = 512


def _rmsnorm_kernel(x_ref, g_ref, o_ref):
    o_ref[...] = _rms_scale(x_ref[...], g_ref[...]).astype(o_ref.dtype)


def _rmsnorm_cast(x, g, layer):
    tm = NORM_TM
    est = 2 * tm * D_MODEL * (4 + 2) + 4 * tm * D_MODEL * 4
    return pl.pallas_call(
        _rmsnorm_kernel,
        grid=(ROWS // tm,),
        in_specs=[pl.BlockSpec((tm, D_MODEL), lambda i: (i, 0)),
                  _layer_spec((1, D_MODEL), layer, 1)],
        out_specs=pl.BlockSpec((tm, D_MODEL), lambda i: (i, 0)),
        out_shape=jax.ShapeDtypeStruct((ROWS, D_MODEL), BF16),
        compiler_params=_compiler_params(("arbitrary",), est),
        name="rmsnorm_cast",
    )(x, _layer_row(g))


CONV_HALO = 32
CONV_ROW_BLOCK = 256


def _shifted_causal_conv(buf_ref, buf_cols, shf_ref, cw_ref, bias, cols, taps, tm, out_ref):
    base = CONV_HALO - (taps - 1)
    span = CONV_HALO + tm - SUBLANES
    residues = sorted({(base + k) % SUBLANES for k in range(taps)} - {0})
    for r in residues:
        shf_ref[r - 1, :, :] = buf_ref[r:r + span, buf_cols]
    for lo in range(0, tm, CONV_ROW_BLOCK):
        acc = bias
        for k in range(taps):
            a, r = divmod(base + k, SUBLANES)
            row = a * SUBLANES + lo
            if r == 0:
                src = buf_ref[row:row + CONV_ROW_BLOCK, buf_cols]
            else:
                src = shf_ref[r - 1, row:row + CONV_ROW_BLOCK, :]
            term = cw_ref[k:k + 1, cols] * src
            acc = term if acc is None else acc + term
        out_ref[lo:lo + CONV_ROW_BLOCK, cols] = acc.astype(out_ref.dtype)


INPROJ_TM = 1024
INPROJ_TN = 1024
INPROJ_QUARTER = INPROJ_TN // 4
INPROJ_ROW_TILES = ROWS // INPROJ_TM
INPROJ_TILES_PER_SEQ = SEQ // INPROJ_TM
PAIR_TILES = CONF_WIDTH // INPROJ_QUARTER
PAIR_CHUNKS = INPROJ_QUARTER // LANES
TILE_POOL = 0
TILE_PAIR = TILE_POOL + POOL_WIDTH // INPROJ_TN
TILE_BG = TILE_PAIR + PAIR_TILES
TILE_GATE = TILE_BG + SCONV_WIDTH // INPROJ_TN
GATE_TILES = GATE_WIDTH // INPROJ_TN
INPROJ_COL_TILES = TILE_GATE + GATE_TILES
Z_POOL_BLOCK = GATE_TILES
Z_BG_BLOCK = Z_POOL_BLOCK + POOL_WIDTH // INPROJ_TN
Z_WIDTH = GATE_WIDTH + POOL_WIDTH + SCONV_WIDTH
CAST_ROWS = 256


def _cast_weight_tile(w_ref, wbf_ref, lane_shift=0):
    rows, cols = wbf_ref.shape
    keep = cols - lane_shift

    def body(r, carry):
        sl = pl.ds(pl.multiple_of(r * CAST_ROWS, CAST_ROWS), CAST_ROWS)
        wbf_ref[sl, 0:keep] = w_ref[sl, lane_shift:cols].astype(BF16)
        if lane_shift:
            wbf_ref[sl, keep:cols] = jnp.zeros((CAST_ROWS, lane_shift), BF16)
        return carry

    lax.fori_loop(0, rows // CAST_ROWS, body, 0)


def _cast_weight_chunks(w_refs, wbf_ref):
    rows = wbf_ref.shape[0]

    def body(r, carry):
        sl = pl.ds(pl.multiple_of(r * CAST_ROWS, CAST_ROWS), CAST_ROWS)
        n = 0
        for c in range(PAIR_CHUNKS):
            src = slice(c * LANES, (c + 1) * LANES)
            for w_ref in w_refs:
                wbf_ref[sl, n * LANES:(n + 1) * LANES] = w_ref[sl, src].astype(BF16)
                n += 1
        return carry

    lax.fori_loop(0, rows // CAST_ROWS, body, 0)


def _inproj_kernel(h_ref, w0_ref, w1_ref, w2_ref, w3_ref, cw_ref, cb_ref, sw_ref, z_ref, y_ref,
                   q_ref, wbf_ref, v_ref, shf_ref):
    tm, quarter = INPROJ_TM, INPROJ_QUARTER
    j, i = pl.program_id(0), pl.program_id(1)
    w_refs = (w0_ref, w1_ref, w2_ref, w3_ref)
    is_pair = (j >= TILE_PAIR) & (j < TILE_PAIR + PAIR_TILES)

    @pl.when(jnp.logical_not(is_pair))
    def _():
        @pl.when(i == 0)
        def _():
            for k, w_ref in enumerate(w_refs):
                _cast_weight_tile(w_ref, wbf_ref.at[:, k * quarter:(k + 1) * quarter])

        z_ref[...] = jnp.dot(h_ref[...], wbf_ref[...],
                             preferred_element_type=F32).astype(z_ref.dtype)

    @pl.when(is_pair)
    def _():
        @pl.when(i == 0)
        def _():
            _cast_weight_chunks(w_refs, wbf_ref)

        @pl.when(i % INPROJ_TILES_PER_SEQ == 0)
        def _():
            v_ref[0:CONV_HALO, :] = jnp.zeros((CONV_HALO, 2 * quarter), F32)

        h = h_ref[...]
        for c in range(PAIR_CHUNKS):
            cols = slice(c * LANES, (c + 1) * LANES)
            for branch in range(2):
                n = 2 * c + branch
                pair = jnp.dot(h, wbf_ref[:, 2 * n * LANES:(2 * n + 2) * LANES],
                               preferred_element_type=F32)
                lo, hi = pair[:, 0:LANES], pair[:, LANES:2 * LANES]
                vcols = slice(branch * quarter + c * LANES, branch * quarter + (c + 1) * LANES)
                if branch == 0:
                    v_ref[CONV_HALO:, vcols] = lo * _sigmoid(hi)
                    _shifted_causal_conv(v_ref, vcols, shf_ref, cw_ref, cb_ref[:, cols], cols,
                                         CONF_KERNEL, tm, y_ref)
                else:
                    v_ref[CONV_HALO:, vcols] = lo * hi
                    _shifted_causal_conv(v_ref, vcols, shf_ref, sw_ref, None, cols,
                                         SCONV_KERNEL, tm, q_ref)
        v_ref[0:CONV_HALO, :] = v_ref[tm:tm + CONV_HALO, :]


def _inproj(h, w_in, conf_w, conf_b, sconv_w, layer):
    tm, tn, quarter = INPROJ_TM, INPROJ_TN, INPROJ_QUARTER
    last_row = INPROJ_ROW_TILES - 1
    pair = lambda j: jnp.clip(j - TILE_PAIR, 0, PAIR_TILES - 1)
    is_pair = lambda j: (j >= TILE_PAIR) & (j < TILE_PAIR + PAIR_TILES)
    paired_qb = (OFF_POOL // quarter, (OFF_POOL + CONF_WIDTH) // quarter,
                 (OFF_CONF + SCONV_WIDTH) // quarter, (OFF_CONF + 2 * SCONV_WIDTH) // quarter)
    bg_qb, gate_qb = OFF_CONF // quarter, OFF_SCONV // quarter

    def plain_qb(j):
        return jnp.where(j < TILE_PAIR, 4 * (j - TILE_POOL),
                         jnp.where(j < TILE_GATE, bg_qb + 4 * (j - TILE_BG),
                                   gate_qb + 4 * (j - TILE_GATE)))

    def w_spec(k):
        return pl.BlockSpec(
            (None, D_MODEL, quarter),
            lambda j, i: (layer, 0, jnp.where(is_pair(j), paired_qb[k] + pair(j), plain_qb(j) + k)))

    def z_index(j, i):
        row = jnp.where(is_pair(j), last_row, i)
        col = jnp.where(j < TILE_BG, Z_POOL_BLOCK,
                        jnp.where(j < TILE_GATE, Z_BG_BLOCK, j - TILE_GATE))
        return (row, col)

    def pair_index(j, i):
        row = jnp.where(j < TILE_PAIR, 0, jnp.where(j >= TILE_PAIR + PAIR_TILES, last_row, i))
        return (row, pair(j))

    est = (2 * tm * D_MODEL * 2 + 2 * 4 * D_MODEL * quarter * 4 + D_MODEL * tn * 2 + 2 * tm * tn * 2
           + 2 * tm * quarter * (4 + 2) + (CONV_HALO + tm) * 2 * quarter * 4
           + (SUBLANES - 1) * (CONV_HALO + tm) * LANES * 4 + 4 * tm * 2 * LANES * 4)
    return pl.pallas_call(
        _inproj_kernel,
        grid=(INPROJ_COL_TILES, INPROJ_ROW_TILES),
        in_specs=[pl.BlockSpec((tm, D_MODEL), lambda j, i: (i, 0)),
                  w_spec(0), w_spec(1), w_spec(2), w_spec(3),
                  pl.BlockSpec((None, CONF_KERNEL, quarter), lambda j, i: (layer, 0, pair(j))),
                  pl.BlockSpec((None, 1, quarter), lambda j, i: (layer, 0, pair(j))),
                  pl.BlockSpec((None, SCONV_KERNEL, quarter), lambda j, i: (layer, 0, pair(j)))],
        out_specs=[pl.BlockSpec((tm, tn), z_index),
                   pl.BlockSpec((tm, quarter), pair_index),
                   pl.BlockSpec((tm, quarter), pair_index)],
        out_shape=[jax.ShapeDtypeStruct((ROWS, Z_WIDTH), BF16),
                   jax.ShapeDtypeStruct((ROWS, CONF_WIDTH), F32),
                   jax.ShapeDtypeStruct((ROWS, SCONV_WIDTH), BF16)],
        scratch_shapes=[pltpu.VMEM((D_MODEL, tn), BF16),
                        pltpu.VMEM((CONV_HALO + tm, 2 * quarter), F32),
                        pltpu.VMEM((SUBLANES - 1, CONV_HALO + tm - SUBLANES, LANES), F32)],
        compiler_params=_compiler_params(("arbitrary", "arbitrary"), est),
        name="inproj",
    )(h, w_in, w_in, w_in, w_in, conf_w, _layer_row(conf_b), sconv_w)


TM_TM = 256
TM_HALO = 32
TM_TILES = ROWS // TM_TM
TM_TILES_PER_SEQ = SEQ // TM_TM
MIX_NC = 512


def _tokenmix_kernel(up_ref, upp_ref, bg_ref, q_ref, y_ref, zg_ref, x_ref, pw_ref, ps_ref, lg_ref,
                     lb_ref, gb_ref, wa_ref, wb_ref, wc_ref, wo_ref, g_ref, xo_ref, ho_ref,
                     fa_ref, fb_ref, fc_ref, buf_ref, mix_ref, pool_ref):
    tm, halo = TM_TM, TM_HALO
    t_idx = pl.program_id(0) % TM_TILES_PER_SEQ
    keep = jnp.where(t_idx == 0, 0.0, 1.0).astype(F32)

    buf_ref[0:halo, :] = upp_ref[...].astype(F32) * keep
    buf_ref[halo:, :] = up_ref[...].astype(F32)
    pos = (t_idx * tm + 1 + lax.broadcasted_iota(jnp.int32, (tm, 1), 0)).astype(F32)
    ext = halo + tm
    pool_ref[0:SUBLANES, :] = jnp.zeros((SUBLANES, POOL_GROUP_DIM), F32)
    for g, w in enumerate(POOL_WINDOWS):
        cols = slice(g * POOL_GROUP_DIM, (g + 1) * POOL_GROUP_DIM)
        ws = buf_ref[:, cols]
        x = ws[halo:, :]
        d = 1
        while d < w:
            pool_ref[d:d + ext, :] = ws
            ws = ws + pool_ref[0:ext, :]
            d *= 2
        inv_count = 1.0 / jnp.minimum(pos, float(w))
        pooled = ws[halo:, :] * inv_count - x
        mixed = jnp.dot(pooled.astype(BF16), pw_ref[g].astype(BF16), preferred_element_type=F32)
        fa_ref[:, cols] = (mixed * ps_ref[:, cols]).astype(fa_ref.dtype)

    fc_ref[...] = (bg_ref[...].astype(F32) * q_ref[...].astype(F32)).astype(fc_ref.dtype)

    y = y_ref[...]
    mu = jnp.mean(y, axis=-1, keepdims=True)
    yc = y - mu
    var = jnp.mean(yc * yc, axis=-1, keepdims=True)
    yn = yc * lax.rsqrt(var + EPS) * lg_ref[...] + lb_ref[...]
    fb_ref[...] = (yn * _sigmoid(yn)).astype(fb_ref.dtype)

    for c in range(D_MODEL // MIX_NC):
        cols = slice(c * MIX_NC, (c + 1) * MIX_NC)
        acc = None
        for k, (f_ref, w_ref) in enumerate(((fa_ref, wa_ref), (fb_ref, wb_ref), (fc_ref, wc_ref))):
            gcols = slice(k * D_MODEL + c * MIX_NC, k * D_MODEL + (c + 1) * MIX_NC)
            gate = _sigmoid(zg_ref[:, gcols].astype(F32) + gb_ref[:, gcols])
            term = gate * jnp.dot(f_ref[...], w_ref[:, cols], preferred_element_type=F32)
            acc = term if acc is None else acc + term
        mix_ref[:, cols] = acc.astype(mix_ref.dtype)

    for c in range(D_MODEL // MIX_NC):
        cols = slice(c * MIX_NC, (c + 1) * MIX_NC)
        xo_ref[:, cols] = x_ref[:, cols] + jnp.dot(mix_ref[...], wo_ref[:, cols],
                                                   preferred_element_type=F32)
    ho_ref[...] = _rms_scale(xo_ref[...], g_ref[...]).astype(ho_ref.dtype)


def _tokenmix(z, y, q, x, layer, pool_w, pool_scale, ln_g, ln_b, gate_b, wa, wb, wc, wo, norm_g):
    tm, halo = TM_TM, TM_HALO
    ratio = tm // halo
    halo_tile = lambda s: jnp.maximum(s * ratio - 1, 0)
    lspec = lambda shape, **kw: _layer_spec(shape, layer, 1, **kw)
    once = dict(pipeline_mode=pl.Buffered(1))
    row_spec = pl.BlockSpec((tm, D_MODEL), lambda s: (s, 0))
    feat = pltpu.VMEM((tm, BRANCH_WIDTH), BF16)
    est = (2 * (tm + halo) * POOL_WIDTH * 2 + 2 * 2 * tm * SCONV_WIDTH * 2 + 2 * tm * CONF_WIDTH * 4
           + 2 * tm * GATE_WIDTH * 2 + 2 * tm * D_MODEL * (4 + 4 + 2)
           + POOL_WIDTH * POOL_GROUP_DIM * 4 + (3 * BRANCH_WIDTH + D_MODEL) * D_MODEL * 2
           + (tm + halo) * BRANCH_WIDTH * 4 + 3 * tm * BRANCH_WIDTH * 2 + tm * D_MODEL * 2
           + 8 * tm * MIX_NC * 4)
    return pl.pallas_call(
        _tokenmix_kernel,
        grid=(TM_TILES,),
        in_specs=[pl.BlockSpec((tm, POOL_WIDTH), lambda s: (s, Z_POOL_BLOCK)),
                  pl.BlockSpec((halo, POOL_WIDTH), lambda s: (halo_tile(s), Z_POOL_BLOCK)),
                  pl.BlockSpec((tm, SCONV_WIDTH), lambda s: (s, Z_BG_BLOCK)),
                  pl.BlockSpec((tm, SCONV_WIDTH), lambda s: (s, 0)),
                  pl.BlockSpec((tm, CONF_WIDTH), lambda s: (s, 0)),
                  pl.BlockSpec((tm, GATE_WIDTH), lambda s: (s, 0)),
                  row_spec,
                  lspec((POOL_GROUPS, POOL_GROUP_DIM, POOL_GROUP_DIM), **once),
                  lspec((1, POOL_WIDTH)),
                  lspec((1, CONF_WIDTH)), lspec((1, CONF_WIDTH)),
                  lspec((1, GATE_WIDTH)),
                  lspec((BRANCH_WIDTH, D_MODEL), **once), lspec((BRANCH_WIDTH, D_MODEL), **once),
                  lspec((BRANCH_WIDTH, D_MODEL), **once), lspec((D_MODEL, D_MODEL), **once),
                  lspec((1, D_MODEL))],
        out_specs=[row_spec, row_spec],
        out_shape=[jax.ShapeDtypeStruct((ROWS, D_MODEL), F32),
                   jax.ShapeDtypeStruct((ROWS, D_MODEL), BF16)],
        scratch_shapes=[feat, feat, feat,
                        pltpu.VMEM((halo + tm, BRANCH_WIDTH), F32),
                        pltpu.VMEM((tm, D_MODEL), BF16),
                        pltpu.VMEM((halo + tm + max(POOL_WINDOWS) // 2, POOL_GROUP_DIM), F32)],
        compiler_params=_compiler_params(("arbitrary",), est),
        name="tokenmix",
    )(z, z, z, q, y, z, x, pool_w, _layer_row(pool_scale), _layer_row(ln_g), _layer_row(ln_b),
      _layer_row(gate_b), wa, wb, wc, wo, _layer_row(norm_g))


UP_TM = 1024
UP_TN = 512
UP_ROW_TILES = ROWS // UP_TM
UP_TILES_PER_SEQ = SEQ // UP_TM
UP_COL_TILES = pl.cdiv(D_FF, UP_TN)
D_FF_PAD = UP_COL_TILES * UP_TN
UP_LAST_SHIFT = D_FF_PAD - D_FF
UP_TAIL = SUBLANES
DOWN_SLAB = 256
DOWN_SLABS_PER_COL_TILE = UP_TN // DOWN_SLAB
DOWN_SLAB_EVERY = UP_ROW_TILES // DOWN_SLABS_PER_COL_TILE
DOWN_LAST_SLAB_ROWS = D_FF - (D_FF_PAD - DOWN_SLAB)


def _up_col_start(j, offset=0):
    tile = jnp.minimum(j * (UP_TN // LANES), (D_FF - UP_TN) // LANES)
    return (offset // LANES + tile) * LANES


def _down_slab(j, i):
    return j * DOWN_SLABS_PER_COL_TILE + i // DOWN_SLAB_EVERY


def _ffn_up_kernel(h_ref, wg_ref, wv_ref, cg_ref, cv_ref, wd_ref, o_ref, wdo_ref,
                   wg_bf, wv_bf, cw_ref, ug_ref, uv_ref):
    tm, tn = UP_TM, UP_TN
    j, i = pl.program_id(0), pl.program_id(1)
    last = UP_COL_TILES - 1

    def stage_weights(shift):
        keep = tn - shift
        _cast_weight_tile(wg_ref.at[0], wg_bf, shift)
        _cast_weight_tile(wv_ref.at[0], wv_bf, shift)
        for r, ref in enumerate((cg_ref, cv_ref)):
            rows = slice(r * SUBLANES, r * SUBLANES + FFN_KERNEL)
            cw_ref[rows, 0:keep] = ref[0, :, shift:tn]
            if shift:
                cw_ref[rows, keep:tn] = jnp.zeros((FFN_KERNEL, shift), F32)

    @pl.when((i == 0) & (j != last))
    def _():
        stage_weights(0)

    @pl.when((i == 0) & (j == last))
    def _():
        stage_weights(UP_LAST_SHIFT)

    def stage_down(rows):
        wdo_ref[0:rows, :] = wd_ref[0:rows, :].astype(BF16)
        if rows < DOWN_SLAB:
            wdo_ref[rows:DOWN_SLAB, :] = jnp.zeros((DOWN_SLAB - rows, D_MODEL), BF16)

    is_visit = i % DOWN_SLAB_EVERY == 0
    is_last_slab = (j == last) & (i // DOWN_SLAB_EVERY == DOWN_SLABS_PER_COL_TILE - 1)

    @pl.when(is_visit & jnp.logical_not(is_last_slab))
    def _():
        stage_down(DOWN_SLAB)

    @pl.when(is_visit & is_last_slab)
    def _():
        stage_down(DOWN_LAST_SLAB_ROWS)

    @pl.when(i % UP_TILES_PER_SEQ == 0)
    def _():
        for u_ref in (ug_ref, uv_ref):
            for k in range(1, FFN_KERNEL):
                u_ref[k, tm:tm + k, :] = jnp.zeros((k, tn), F32)

    for u_ref in (ug_ref, uv_ref):
        for k in range(1, FFN_KERNEL):
            u_ref[k, 0:k, :] = u_ref[k, tm:tm + k, :]

    h = h_ref[...]
    for u_ref, w_bf in ((ug_ref, wg_bf), (uv_ref, wv_bf)):
        up = jnp.dot(h, w_bf[...], preferred_element_type=F32)
        for k in range(FFN_KERNEL):
            u_ref[k, k:k + tm, :] = up

    def conv(u_ref, w_row):
        acc = None
        for k in range(FFN_KERNEL):
            src = u_ref[FFN_KERNEL - 1 - k, 0:tm, :]
            term = cw_ref[w_row + k:w_row + k + 1, :] * src
            acc = term if acc is None else acc + term
        return acc

    gt = conv(ug_ref, 0)
    vl = conv(uv_ref, SUBLANES)
    o_ref[...] = (gt * _sigmoid(gt) * vl).astype(o_ref.dtype)


def _ffn_up(h, ffn_up, ffn_conv_w, ffn_down, layer):
    tm, tn = UP_TM, UP_TN
    w_spec = lambda off: pl.BlockSpec((pl.Element(1), pl.Element(D_MODEL), pl.Element(tn)),
                                      lambda j, i: (layer, 0, _up_col_start(j, off)))
    cw_spec = lambda off: pl.BlockSpec((pl.Element(1), pl.Element(FFN_KERNEL), pl.Element(tn)),
                                       lambda j, i: (layer, 0, _up_col_start(j, off)))
    u_scratch = pltpu.VMEM((FFN_KERNEL, tm + UP_TAIL, tn), F32)
    est = (2 * tm * D_MODEL * 2 + 2 * 2 * D_MODEL * tn * 4 + 2 * D_MODEL * tn * 2 + 2 * tm * tn * 2
           + 2 * FFN_KERNEL * (tm + UP_TAIL) * tn * 4 + 2 * DOWN_SLAB * D_MODEL * (4 + 2)
           + 4 * tm * tn * 4)
    return pl.pallas_call(
        _ffn_up_kernel,
        grid=(UP_COL_TILES, UP_ROW_TILES),
        in_specs=[pl.BlockSpec((tm, D_MODEL), lambda j, i: (i, 0)),
                  w_spec(0), w_spec(D_FF), cw_spec(0), cw_spec(D_FF),
                  pl.BlockSpec((None, DOWN_SLAB, D_MODEL),
                               lambda j, i: (layer, _down_slab(j, i), 0))],
        out_specs=[pl.BlockSpec((tm, tn), lambda j, i: (i, j)),
                   pl.BlockSpec((DOWN_SLAB, D_MODEL), lambda j, i: (_down_slab(j, i), 0))],
        out_shape=[jax.ShapeDtypeStruct((ROWS, D_FF_PAD), BF16),
                   jax.ShapeDtypeStruct((D_FF_PAD, D_MODEL), BF16)],
        scratch_shapes=[pltpu.VMEM((D_MODEL, tn), BF16), pltpu.VMEM((D_MODEL, tn), BF16),
                        pltpu.VMEM((2 * SUBLANES, tn), F32), u_scratch, u_scratch],
        compiler_params=_compiler_params(("arbitrary", "arbitrary"), est),
        name="ffn_up",
    )(h, ffn_up, ffn_up, ffn_conv_w, ffn_conv_w, ffn_down)


DOWN_TM = 512


def _ffn_down_kernel(act_ref, w_ref, x_ref, g_ref, *out_refs, last):
    xn = x_ref[...] + jnp.dot(act_ref[...], w_ref[...], preferred_element_type=F32)
    normed = _rms_scale(xn, g_ref[...])
    if last:
        (y_ref,) = out_refs
        y_ref[...] = normed
    else:
        xo_ref, ho_ref = out_refs
        xo_ref[...] = xn
        ho_ref[...] = normed.astype(ho_ref.dtype)


def _ffn_down(act, wd, x, g, last):
    tm = DOWN_TM
    row_spec = pl.BlockSpec((tm, D_MODEL), lambda i: (i, 0))
    if last:
        out_specs = [row_spec]
        out_shape = [jax.ShapeDtypeStruct((ROWS, D_MODEL), F32)]
    else:
        out_specs = [row_spec, row_spec]
        out_shape = [jax.ShapeDtypeStruct((ROWS, D_MODEL), F32),
                     jax.ShapeDtypeStruct((ROWS, D_MODEL), BF16)]
    est = (2 * tm * D_FF_PAD * 2 + D_FF_PAD * D_MODEL * 2 + 2 * tm * D_MODEL * (4 + 4 + 2)
           + 3 * tm * D_MODEL * 4)
    return pl.pallas_call(
        functools.partial(_ffn_down_kernel, last=last),
        grid=(ROWS // tm,),
        in_specs=[pl.BlockSpec((tm, D_FF_PAD), lambda i: (i, 0)),
                  pl.BlockSpec((D_FF_PAD, D_MODEL), lambda i: (0, 0), pipeline_mode=pl.Buffered(1)),
                  row_spec,
                  pl.BlockSpec((1, D_MODEL), lambda i: (0, 0))],
        out_specs=out_specs,
        out_shape=out_shape,
        compiler_params=_compiler_params(("arbitrary",), est),
        name="ffn_down",
    )(act, wd, x, g.reshape(1, -1))


def kernel(x, norm1_g, w_in, gate_b, pool_w, pool_scale, pool_proj, conf_conv_w, conf_conv_b,
           conf_ln_g, conf_ln_b, conf_proj, sconv_w, sconv_proj, w_o, norm2_g, ffn_up, ffn_conv_w,
           ffn_down, final_g):
    assert x.shape == (BATCH, SEQ, D_MODEL) and w_in.shape == (DEPTH, D_MODEL, D_IN)
    xs = x.reshape(ROWS, D_MODEL)
    wa, wb, wc, wo = (w.astype(BF16) for w in (pool_proj, conf_proj, sconv_proj, w_o))
    h = _rmsnorm_cast(xs, norm1_g, 0)
    for l in range(DEPTH):
        z, conv_y, conv_q = _inproj(h, w_in, conf_conv_w, conf_conv_b, sconv_w, l)
        xs, h = _tokenmix(z, conv_y, conv_q, xs, l, pool_w, pool_scale, conf_ln_g, conf_ln_b, gate_b,
                          wa, wb, wc, wo, norm2_g)
        act, wd = _ffn_up(h, ffn_up, ffn_conv_w, ffn_down, l)
        last = l == DEPTH - 1
        g_next = final_g if last else norm1_g[l + 1]
        outs = _ffn_down(act, wd, xs, g_next, last)
        if last:
            (y,) = outs
        else:
            xs, h = outs
    return y.reshape(BATCH, SEQ, D_MODEL)
```

```python
import functools

import jax
import jax.numpy as jnp
from jax import lax
from jax.experimental import pallas as pl
from jax.experimental.pallas import tpu as pltpu

D_MODEL = 2048
BATCH = 4
SEQ = 2048
DEPTH = 2
ROWS = BATCH * SEQ

POOL_WIDTH = 1024
POOL_GROUPS = 4
POOL_WINDOWS = (2, 4, 8, 16)
POOL_GROUP_DIM = POOL_WIDTH // POOL_GROUPS
CONF_WIDTH = 1024
CONF_KERNEL = 31
SCONV_WIDTH = 1024
SCONV_KERNEL = 3
N_BRANCH = 3
BRANCH_WIDTH = 1024
OFF_POOL = POOL_WIDTH
OFF_CONF = OFF_POOL + 2 * CONF_WIDTH
OFF_SCONV = OFF_CONF + 3 * SCONV_WIDTH
GATE_WIDTH = N_BRANCH * D_MODEL
D_IN = OFF_SCONV + GATE_WIDTH
D_FF = 5504
FFN_KERNEL = 3
EPS = 1e-6

V7X_VMEM_BYTES = 64 * 1024 * 1024
SUBLANES = 8
LANES = 128

BF16 = jnp.bfloat16
F32 = jnp.float32


def _compiler_params(semantics, vmem_estimate_bytes):
    limit = min(int(vmem_estimate_bytes * 1.25) + (4 << 20), V7X_VMEM_BYTES - (4 << 20))
    return pltpu.CompilerParams(dimension_semantics=semantics, vmem_limit_bytes=limit)


def _sigmoid(x):
    return 1.0 / (1.0 + jnp.exp(-x))


def _rms_scale(x, g):
    ms = jnp.mean(x * x, axis=-1, keepdims=True)
    return x * lax.rsqrt(ms + EPS) * g


def _layer_row(stacked):
    return stacked.reshape(stacked.shape[0], 1, stacked.shape[1])


def _layer_spec(shape, layer, grid_rank, **kwargs):
    zeros = (0,) * len(shape)
    if grid_rank == 1:
        index_map = lambda i: (layer,) + zeros
    else:
        index_map = lambda j, i: (layer,) + zeros
    return pl.BlockSpec((None,) + tuple(shape), index_map, **kwargs)


NORM_TM = 512


def _rmsnorm_kernel(x_ref, g_ref, o_ref):
    o_ref[...] = _rms_scale(x_ref[...], g_ref[...]).astype(o_ref.dtype)


def _rmsnorm_cast(x, g, layer):
    tm = NORM_TM
    est = 2 * tm * D_MODEL * (4 + 2) + 4 * tm * D_MODEL * 4
    return pl.pallas_call(
        _rmsnorm_kernel,
        grid=(ROWS // tm,),
        in_specs=[pl.BlockSpec((tm, D_MODEL), lambda i: (i, 0)),
                  _layer_spec((1, D_MODEL), layer, 1)],
        out_specs=pl.BlockSpec((tm, D_MODEL), lambda i: (i, 0)),
        out_shape=jax.ShapeDtypeStruct((ROWS, D_MODEL), BF16),
        compiler_params=_compiler_params(("arbitrary",), est),
        name="rmsnorm_cast",
    )(x, _layer_row(g))


CONV_HALO = 32
CONV_ROW_BLOCK = 256


def _shifted_causal_conv(buf_ref, buf_cols, shf_ref, cw_ref, bias, cols, taps, tm, out_ref):
    base = CONV_HALO - (taps - 1)
    span = CONV_HALO + tm - SUBLANES
    residues = sorted({(base + k) % SUBLANES for k in range(taps)} - {0})
    for r in residues:
        shf_ref[r - 1, :, :] = buf_ref[r:r + span, buf_cols]
    for lo in range(0, tm, CONV_ROW_BLOCK):
        acc = bias
        for k in range(taps):
            a, r = divmod(base + k, SUBLANES)
            row = a * SUBLANES + lo
            if r == 0:
                src = buf_ref[row:row + CONV_ROW_BLOCK, buf_cols]
            else:
                src = shf_ref[r - 1, row:row + CONV_ROW_BLOCK, :]
            term = cw_ref[k:k + 1, cols] * src
            acc = term if acc is None else acc + term
        out_ref[lo:lo + CONV_ROW_BLOCK, cols] = acc.astype(out_ref.dtype)


INPROJ_TM = 1024
INPROJ_TN = 1024
INPROJ_HALF = INPROJ_TN // 2
INPROJ_ROW_TILES = ROWS // INPROJ_TM
INPROJ_TILES_PER_SEQ = SEQ // INPROJ_TM
MIX_TILES = CONF_WIDTH // LANES
MIX_GATE_WIDTH = MIX_TILES * INPROJ_HALF
TILE_POOL = 0
TILE_MIX = TILE_POOL + POOL_WIDTH // INPROJ_TN
TILE_BG = TILE_MIX + MIX_TILES
TILE_GATE = TILE_BG + SCONV_WIDTH // INPROJ_TN
GATE_TILES = (GATE_WIDTH - MIX_GATE_WIDTH) // INPROJ_TN
INPROJ_COL_TILES = TILE_GATE + GATE_TILES
Z_POOL_BLOCK = GATE_TILES
Z_BG_BLOCK = Z_POOL_BLOCK + POOL_WIDTH // INPROJ_TN
Z_WIDTH = GATE_WIDTH - MIX_GATE_WIDTH + POOL_WIDTH + SCONV_WIDTH
CAST_ROWS = 256


def _cast_weight_tile(w_ref, wbf_ref, lane_shift=0):
    rows, cols = wbf_ref.shape
    keep = cols - lane_shift

    def body(r, carry):
        sl = pl.ds(pl.multiple_of(r * CAST_ROWS, CAST_ROWS), CAST_ROWS)
        wbf_ref[sl, 0:keep] = w_ref[sl, lane_shift:cols].astype(BF16)
        if lane_shift:
            wbf_ref[sl, keep:cols] = jnp.zeros((CAST_ROWS, lane_shift), BF16)
        return carry

    lax.fori_loop(0, rows // CAST_ROWS, body, 0)


def _inproj_kernel(h_ref, w0_ref, w1_ref, w2_ref, w3_ref, w4_ref, cw_ref, cb_ref, sw_ref,
                   z_ref, zg_ref, y_ref, q_ref, wbf_ref, v_ref, shf_ref, g_ref):
    tm, half = INPROJ_TM, INPROJ_HALF
    j, i = pl.program_id(0), pl.program_id(1)
    is_mix = (j >= TILE_MIX) & (j < TILE_MIX + MIX_TILES)

    @pl.when(i == 0)
    def _():
        for k, w_ref in enumerate((w0_ref, w1_ref, w2_ref, w3_ref)):
            _cast_weight_tile(w_ref, wbf_ref.at[:, k * LANES:(k + 1) * LANES])
        _cast_weight_tile(w4_ref, wbf_ref.at[:, half:2 * half])

    @pl.when(jnp.logical_not(is_mix))
    def _():
        z_ref[...] = jnp.dot(h_ref[...], wbf_ref[...],
                             preferred_element_type=F32).astype(z_ref.dtype)

    @pl.when(is_mix)
    def _():
        @pl.when(i % INPROJ_TILES_PER_SEQ == 0)
        def _():
            v_ref[0:CONV_HALO, :] = jnp.zeros((CONV_HALO, 2 * LANES), F32)

        h = h_ref[...]
        cols = slice(0, LANES)
        pair = jnp.dot(h, wbf_ref[:, 0:2 * LANES], preferred_element_type=F32)
        v_ref[CONV_HALO:, cols] = pair[:, 0:LANES] * _sigmoid(pair[:, LANES:2 * LANES])
        _shifted_causal_conv(v_ref, cols, shf_ref, cw_ref, cb_ref[...], cols, CONF_KERNEL, tm,
                             y_ref)
        for c in range(half // (2 * LANES)):
            gcols = slice(c * 2 * LANES, (c + 1) * 2 * LANES)
            g_ref[:, gcols] = jnp.dot(h, wbf_ref[:, half + c * 2 * LANES:half + (c + 1) * 2 * LANES],
                                      preferred_element_type=F32)
        pair = jnp.dot(h, wbf_ref[:, 2 * LANES:4 * LANES], preferred_element_type=F32)
        vcols = slice(LANES, 2 * LANES)
        v_ref[CONV_HALO:, vcols] = pair[:, 0:LANES] * pair[:, LANES:2 * LANES]
        _shifted_causal_conv(v_ref, vcols, shf_ref, sw_ref, None, cols, SCONV_KERNEL, tm, q_ref)
        zg_ref[...] = g_ref[...].astype(zg_ref.dtype)
        v_ref[0:CONV_HALO, :] = v_ref[tm:tm + CONV_HALO, :]


def _inproj(h, w_in, conf_w, conf_b, sconv_w, layer):
    tm, tn, half = INPROJ_TM, INPROJ_TN, INPROJ_HALF
    last_row = INPROJ_ROW_TILES - 1
    mix = lambda j: jnp.clip(j - TILE_MIX, 0, MIX_TILES - 1)
    is_mix = lambda j: (j >= TILE_MIX) & (j < TILE_MIX + MIX_TILES)
    mixed_col = (OFF_POOL, OFF_POOL + CONF_WIDTH, OFF_CONF + SCONV_WIDTH, OFF_CONF + 2 * SCONV_WIDTH)

    def plain_col(j):
        return jnp.where(j < TILE_MIX, tn * (j - TILE_POOL),
                         jnp.where(j < TILE_GATE, OFF_CONF + tn * (j - TILE_BG),
                                   OFF_SCONV + MIX_GATE_WIDTH + tn * (j - TILE_GATE)))

    def w128_spec(k):
        return pl.BlockSpec(
            (None, D_MODEL, LANES),
            lambda j, i: (layer, 0, jnp.where(is_mix(j), mixed_col[k] // LANES + mix(j),
                                             plain_col(j) // LANES + k)))

    w512_spec = pl.BlockSpec(
        (None, D_MODEL, half),
        lambda j, i: (layer, 0, jnp.where(is_mix(j), OFF_SCONV // half + mix(j),
                                         plain_col(j) // half + 1)))

    def z_index(j, i):
        row = jnp.where(is_mix(j), last_row, i)
        col = jnp.where(j < TILE_BG, Z_POOL_BLOCK,
                        jnp.where(j < TILE_GATE, Z_BG_BLOCK, j - TILE_GATE))
        return (row, col)

    def mix_index(j, i):
        row = jnp.where(j < TILE_MIX, 0, jnp.where(j >= TILE_MIX + MIX_TILES, last_row, i))
        return (row, mix(j))

    est = (2 * tm * D_MODEL * 2 + 2 * D_MODEL * tn * 4 + D_MODEL * tn * 2 + 2 * tm * tn * 2
           + 2 * tm * half * 2 + 2 * tm * LANES * (4 + 2) + (CONV_HALO + tm) * 2 * LANES * 4
           + (SUBLANES - 1) * (CONV_HALO + tm) * LANES * 4 + tm * half * 4
           + 4 * tm * 2 * LANES * 4)
    return pl.pallas_call(
        _inproj_kernel,
        grid=(INPROJ_COL_TILES, INPROJ_ROW_TILES),
        in_specs=[pl.BlockSpec((tm, D_MODEL), lambda j, i: (i, 0)),
                  w128_spec(0), w128_spec(1), w128_spec(2), w128_spec(3), w512_spec,
                  pl.BlockSpec((None, CONF_KERNEL, LANES), lambda j, i: (layer, 0, mix(j))),
                  pl.BlockSpec((None, 1, LANES), lambda j, i: (layer, 0, mix(j))),
                  pl.BlockSpec((None, SCONV_KERNEL, LANES), lambda j, i: (layer, 0, mix(j)))],
        out_specs=[pl.BlockSpec((tm, tn), z_index),
                   pl.BlockSpec((tm, half), mix_index),
                   pl.BlockSpec((tm, LANES), mix_index),
                   pl.BlockSpec((tm, LANES), mix_index)],
        out_shape=[jax.ShapeDtypeStruct((ROWS, Z_WIDTH), BF16),
                   jax.ShapeDtypeStruct((ROWS, MIX_GATE_WIDTH), BF16),
                   jax.ShapeDtypeStruct((ROWS, CONF_WIDTH), F32),
                   jax.ShapeDtypeStruct((ROWS, SCONV_WIDTH), BF16)],
        scratch_shapes=[pltpu.VMEM((D_MODEL, tn), BF16),
                        pltpu.VMEM((CONV_HALO + tm, 2 * LANES), F32),
                        pltpu.VMEM((SUBLANES - 1, CONV_HALO + tm - SUBLANES, LANES), F32),
                        pltpu.VMEM((tm, half), F32)],
        compiler_params=_compiler_params(("arbitrary", "arbitrary"), est),
        name="inproj",
    )(h, w_in, w_in, w_in, w_in, w_in, conf_w, _layer_row(conf_b), sconv_w)


TM_TM = 256
TM_HALO = 32
TM_TILES = ROWS // TM_TM
TM_TILES_PER_SEQ = SEQ // TM_TM
MIX_NC = 512


def _tokenmix_kernel(up_ref, upp_ref, bg_ref, q_ref, y_ref, zga_ref, zgb_ref, x_ref, pw_ref, ps_ref,
                     lg_ref, lb_ref, gb_ref, wa_ref, wb_ref, wc_ref, wo_ref, g_ref, xo_ref, ho_ref,
                     fa_ref, fb_ref, fc_ref, buf_ref, mix_ref, pool_ref):
    tm, halo = TM_TM, TM_HALO
    t_idx = pl.program_id(0) % TM_TILES_PER_SEQ
    keep = jnp.where(t_idx == 0, 0.0, 1.0).astype(F32)

    buf_ref[0:halo, :] = upp_ref[...].astype(F32) * keep
    buf_ref[halo:, :] = up_ref[...].astype(F32)
    pos = (t_idx * tm + 1 + lax.broadcasted_iota(jnp.int32, (tm, 1), 0)).astype(F32)
    ext = halo + tm
    pool_ref[0:SUBLANES, :] = jnp.zeros((SUBLANES, POOL_GROUP_DIM), F32)
    for g, w in enumerate(POOL_WINDOWS):
        cols = slice(g * POOL_GROUP_DIM, (g + 1) * POOL_GROUP_DIM)
        ws = buf_ref[:, cols]
        x = ws[halo:, :]
        d = 1
        while d < w:
            pool_ref[d:d + ext, :] = ws
            ws = ws + pool_ref[0:ext, :]
            d *= 2
        inv_count = 1.0 / jnp.minimum(pos, float(w))
        pooled = ws[halo:, :] * inv_count - x
        mixed = jnp.dot(pooled.astype(BF16), pw_ref[g].astype(BF16), preferred_element_type=F32)
        fa_ref[:, cols] = (mixed * ps_ref[:, cols]).astype(fa_ref.dtype)

    fc_ref[...] = (bg_ref[...].astype(F32) * q_ref[...].astype(F32)).astype(fc_ref.dtype)

    y = y_ref[...]
    mu = jnp.mean(y, axis=-1, keepdims=True)
    yc = y - mu
    var = jnp.mean(yc * yc, axis=-1, keepdims=True)
    yn = yc * lax.rsqrt(var + EPS) * lg_ref[...] + lb_ref[...]
    fb_ref[...] = (yn * _sigmoid(yn)).astype(fb_ref.dtype)

    for c in range(D_MODEL // MIX_NC):
        cols = slice(c * MIX_NC, (c + 1) * MIX_NC)
        acc = None
        for k, (f_ref, w_ref) in enumerate(((fa_ref, wa_ref), (fb_ref, wb_ref), (fc_ref, wc_ref))):
            gcols = slice(k * D_MODEL + c * MIX_NC, k * D_MODEL + (c + 1) * MIX_NC)
            if gcols.stop <= MIX_GATE_WIDTH:
                logits = zga_ref[:, gcols]
            else:
                logits = zgb_ref[:, gcols.start - MIX_GATE_WIDTH:gcols.stop - MIX_GATE_WIDTH]
            gate = _sigmoid(logits.astype(F32) + gb_ref[:, gcols])
            term = gate * jnp.dot(f_ref[...], w_ref[:, cols], preferred_element_type=F32)
            acc = term if acc is None else acc + term
        mix_ref[:, cols] = acc.astype(mix_ref.dtype)

    for c in range(D_MODEL // MIX_NC):
        cols = slice(c * MIX_NC, (c + 1) * MIX_NC)
        xo_ref[:, cols] = x_ref[:, cols] + jnp.dot(mix_ref[...], wo_ref[:, cols],
                                                   preferred_element_type=F32)
    ho_ref[...] = _rms_scale(xo_ref[...], g_ref[...]).astype(ho_ref.dtype)


def _tokenmix(z, zg, y, q, x, layer, pool_w, pool_scale, ln_g, ln_b, gate_b, wa, wb, wc, wo, norm_g):
    tm, halo = TM_TM, TM_HALO
    ratio = tm // halo
    halo_tile = lambda s: jnp.maximum(s * ratio - 1, 0)
    lspec = lambda shape, **kw: _layer_spec(shape, layer, 1, **kw)
    once = dict(pipeline_mode=pl.Buffered(1))
    row_spec = pl.BlockSpec((tm, D_MODEL), lambda s: (s, 0))
    feat = pltpu.VMEM((tm, BRANCH_WIDTH), BF16)
    est = (2 * (tm + halo) * POOL_WIDTH * 2 + 2 * 2 * tm * SCONV_WIDTH * 2 + 2 * tm * CONF_WIDTH * 4
           + 2 * tm * GATE_WIDTH * 2 + 2 * tm * D_MODEL * (4 + 4 + 2)
           + POOL_WIDTH * POOL_GROUP_DIM * 4 + (3 * BRANCH_WIDTH + D_MODEL) * D_MODEL * 2
           + (tm + halo) * BRANCH_WIDTH * 4 + 3 * tm * BRANCH_WIDTH * 2 + tm * D_MODEL * 2
           + 8 * tm * MIX_NC * 4)
    return pl.pallas_call(
        _tokenmix_kernel,
        grid=(TM_TILES,),
        in_specs=[pl.BlockSpec((tm, POOL_WIDTH), lambda s: (s, Z_POOL_BLOCK)),
                  pl.BlockSpec((halo, POOL_WIDTH), lambda s: (halo_tile(s), Z_POOL_BLOCK)),
                  pl.BlockSpec((tm, SCONV_WIDTH), lambda s: (s, Z_BG_BLOCK)),
                  pl.BlockSpec((tm, SCONV_WIDTH), lambda s: (s, 0)),
                  pl.BlockSpec((tm, CONF_WIDTH), lambda s: (s, 0)),
                  pl.BlockSpec((tm, MIX_GATE_WIDTH), lambda s: (s, 0)),
                  pl.BlockSpec((tm, GATE_WIDTH - MIX_GATE_WIDTH), lambda s: (s, 0)),
                  row_spec,
                  lspec((POOL_GROUPS, POOL_GROUP_DIM, POOL_GROUP_DIM), **once),
                  lspec((1, POOL_WIDTH)),
                  lspec((1, CONF_WIDTH)), lspec((1, CONF_WIDTH)),
                  lspec((1, GATE_WIDTH)),
                  lspec((BRANCH_WIDTH, D_MODEL), **once), lspec((BRANCH_WIDTH, D_MODEL), **once),
                  lspec((BRANCH_WIDTH, D_MODEL), **once), lspec((D_MODEL, D_MODEL), **once),
                  lspec((1, D_MODEL))],
        out_specs=[row_spec, row_spec],
        out_shape=[jax.ShapeDtypeStruct((ROWS, D_MODEL), F32),
                   jax.ShapeDtypeStruct((ROWS, D_MODEL), BF16)],
        scratch_shapes=[feat, feat, feat,
                        pltpu.VMEM((halo + tm, BRANCH_WIDTH), F32),
                        pltpu.VMEM((tm, D_MODEL), BF16),
                        pltpu.VMEM((halo + tm + max(POOL_WINDOWS) // 2, POOL_GROUP_DIM), F32)],
        compiler_params=_compiler_params(("arbitrary",), est),
        name="tokenmix",
    )(z, z, z, q, y, zg, z, x, pool_w, _layer_row(pool_scale), _layer_row(ln_g), _layer_row(ln_b),
      _layer_row(gate_b), wa, wb, wc, wo, _layer_row(norm_g))


UP_TM = 1024
UP_TN = 512
UP_ROW_TILES = ROWS // UP_TM
UP_TILES_PER_SEQ = SEQ // UP_TM
UP_COL_TILES = pl.cdiv(D_FF, UP_TN)
D_FF_PAD = UP_COL_TILES * UP_TN
UP_LAST_SHIFT = D_FF_PAD - D_FF
UP_TAIL = SUBLANES
DOWN_SLAB = 256
DOWN_SLABS_PER_COL_TILE = UP_TN // DOWN_SLAB
DOWN_SLAB_EVERY = UP_ROW_TILES // DOWN_SLABS_PER_COL_TILE
DOWN_LAST_SLAB_ROWS = D_FF - (D_FF_PAD - DOWN_SLAB)


def _up_col_start(j, offset=0):
    tile = jnp.minimum(j * (UP_TN // LANES), (D_FF - UP_TN) // LANES)
    return (offset // LANES + tile) * LANES


def _down_slab(j, i):
    return j * DOWN_SLABS_PER_COL_TILE + i // DOWN_SLAB_EVERY


def _ffn_up_kernel(h_ref, wg_ref, wv_ref, cg_ref, cv_ref, wd_ref, o_ref, wdo_ref,
                   wg_bf, wv_bf, cw_ref, ug_ref, uv_ref):
    tm, tn = UP_TM, UP_TN
    j, i = pl.program_id(0), pl.program_id(1)
    last = UP_COL_TILES - 1

    def stage_weights(shift):
        keep = tn - shift
        _cast_weight_tile(wg_ref.at[0], wg_bf, shift)
        _cast_weight_tile(wv_ref.at[0], wv_bf, shift)
        for r, ref in enumerate((cg_ref, cv_ref)):
            rows = slice(r * SUBLANES, r * SUBLANES + FFN_KERNEL)
            cw_ref[rows, 0:keep] = ref[0, :, shift:tn]
            if shift:
                cw_ref[rows, keep:tn] = jnp.zeros((FFN_KERNEL, shift), F32)

    @pl.when((i == 0) & (j != last))
    def _():
        stage_weights(0)

    @pl.when((i == 0) & (j == last))
    def _():
        stage_weights(UP_LAST_SHIFT)

    def stage_down(rows):
        wdo_ref[0:rows, :] = wd_ref[0:rows, :].astype(BF16)
        if rows < DOWN_SLAB:
            wdo_ref[rows:DOWN_SLAB, :] = jnp.zeros((DOWN_SLAB - rows, D_MODEL), BF16)

    is_visit = i % DOWN_SLAB_EVERY == 0
    is_last_slab = (j == last) & (i // DOWN_SLAB_EVERY == DOWN_SLABS_PER_COL_TILE - 1)

    @pl.when(is_visit & jnp.logical_not(is_last_slab))
    def _():
        stage_down(DOWN_SLAB)

    @pl.when(is_visit & is_last_slab)
    def _():
        stage_down(DOWN_LAST_SLAB_ROWS)

    @pl.when(i % UP_TILES_PER_SEQ == 0)
    def _():
        for u_ref in (ug_ref, uv_ref):
            for k in range(1, FFN_KERNEL):
                u_ref[k, tm:tm + k, :] = jnp.zeros((k, tn), F32)

    for u_ref in (ug_ref, uv_ref):
        for k in range(1, FFN_KERNEL):
            u_ref[k, 0:k, :] = u_ref[k, tm:tm + k, :]

    h = h_ref[...]
    for u_ref, w_bf in ((ug_ref, wg_bf), (uv_ref, wv_bf)):
        up = jnp.dot(h, w_bf[...], preferred_element_type=F32)
        for k in range(FFN_KERNEL):
            u_ref[k, k:k + tm, :] = up

    def conv(u_ref, w_row):
        acc = None
        for k in range(FFN_KERNEL):
            src = u_ref[FFN_KERNEL - 1 - k, 0:tm, :]
            term = cw_ref[w_row + k:w_row + k + 1, :] * src
            acc = term if acc is None else acc + term
        return acc

    gt = conv(ug_ref, 0)
    vl = conv(uv_ref, SUBLANES)
    o_ref[...] = (gt * _sigmoid(gt) * vl).astype(o_ref.dtype)


def _ffn_up(h, ffn_up, ffn_conv_w, ffn_down, layer):
    tm, tn = UP_TM, UP_TN
    w_spec = lambda off: pl.BlockSpec((pl.Element(1), pl.Element(D_MODEL), pl.Element(tn)),
                                      lambda j, i: (layer, 0, _up_col_start(j, off)))
    cw_spec = lambda off: pl.BlockSpec((pl.Element(1), pl.Element(FFN_KERNEL), pl.Element(tn)),
                                       lambda j, i: (layer, 0, _up_col_start(j, off)))
    u_scratch = pltpu.VMEM((FFN_KERNEL, tm + UP_TAIL, tn), F32)
    est = (2 * tm * D_MODEL * 2 + 2 * 2 * D_MODEL * tn * 4 + 2 * D_MODEL * tn * 2 + 2 * tm * tn * 2
           + 2 * FFN_KERNEL * (tm + UP_TAIL) * tn * 4 + 2 * DOWN_SLAB * D_MODEL * (4 + 2)
           + 4 * tm * tn * 4)
    return pl.pallas_call(
        _ffn_up_kernel,
        grid=(UP_COL_TILES, UP_ROW_TILES),
        in_specs=[pl.BlockSpec((tm, D_MODEL), lambda j, i: (i, 0)),
                  w_spec(0), w_spec(D_FF), cw_spec(0), cw_spec(D_FF),
                  pl.BlockSpec((None, DOWN_SLAB, D_MODEL),
                               lambda j, i: (layer, _down_slab(j, i), 0))],
        out_specs=[pl.BlockSpec((tm, tn), lambda j, i: (i, j)),
                   pl.BlockSpec((DOWN_SLAB, D_MODEL), lambda j, i: (_down_slab(j, i), 0))],
        out_shape=[jax.ShapeDtypeStruct((ROWS, D_FF_PAD), BF16),
                   jax.ShapeDtypeStruct((D_FF_PAD, D_MODEL), BF16)],
        scratch_shapes=[pltpu.VMEM((D_MODEL, tn), BF16), pltpu.VMEM((D_MODEL, tn), BF16),
                        pltpu.VMEM((2 * SUBLANES, tn), F32), u_scratch, u_scratch],
        compiler_params=_compiler_params(("arbitrary", "arbitrary"), est),
        name="ffn_up",
    )(h, ffn_up, ffn_up, ffn_conv_w, ffn_conv_w, ffn_down)


DOWN_TM = 512


def _ffn_down_kernel(act_ref, w_ref, x_ref, g_ref, *out_refs, last):
    xn = x_ref[...] + jnp.dot(act_ref[...], w_ref[...], preferred_element_type=F32)
    normed = _rms_scale(xn, g_ref[...])
    if last:
        (y_ref,) = out_refs
        y_ref[...] = normed
    else:
        xo_ref, ho_ref = out_refs
        xo_ref[...] = xn
        ho_ref[...] = normed.astype(ho_ref.dtype)


def _ffn_down(act, wd, x, g, last):
    tm = DOWN_TM
    row_spec = pl.BlockSpec((tm, D_MODEL), lambda i: (i, 0))
    if last:
        out_specs = [row_spec]
        out_shape = [jax.ShapeDtypeStruct((ROWS, D_MODEL), F32)]
    else:
        out_specs = [row_spec, row_spec]
        out_shape = [jax.ShapeDtypeStruct((ROWS, D_MODEL), F32),
                     jax.ShapeDtypeStruct((ROWS, D_MODEL), BF16)]
    est = (2 * tm * D_FF_PAD * 2 + D_FF_PAD * D_MODEL * 2 + 2 * tm * D_MODEL * (4 + 4 + 2)
           + 3 * tm * D_MODEL * 4)
    return pl.pallas_call(
        functools.partial(_ffn_down_kernel, last=last),
        grid=(ROWS // tm,),
        in_specs=[pl.BlockSpec((tm, D_FF_PAD), lambda i: (i, 0)),
                  pl.BlockSpec((D_FF_PAD, D_MODEL), lambda i: (0, 0), pipeline_mode=pl.Buffered(1)),
                  row_spec,
                  pl.BlockSpec((1, D_MODEL), lambda i: (0, 0))],
        out_specs=out_specs,
        out_shape=out_shape,
        compiler_params=_compiler_params(("arbitrary",), est),
        name="ffn_down",
    )(act, wd, x, g.reshape(1, -1))


def kernel(x, norm1_g, w_in, gate_b, pool_w, pool_scale, pool_proj, conf_conv_w, conf_conv_b,
           conf_ln_g, conf_ln_b, conf_proj, sconv_w, sconv_proj, w_o, norm2_g, ffn_up, ffn_conv_w,
           ffn_down, final_g):
    assert x.shape == (BATCH, SEQ, D_MODEL) and w_in.shape == (DEPTH, D_MODEL, D_IN)
    xs = x.reshape(ROWS, D_MODEL)
    wa, wb, wc, wo = (w.astype(BF16) for w in (pool_proj, conf_proj, sconv_proj, w_o))
    h = _rmsnorm_cast(xs, norm1_g, 0)
    for l in range(DEPTH):
        z, zg, conv_y, conv_q = _inproj(h, w_in, conf_conv_w, conf_conv_b, sconv_w, l)
        xs, h = _tokenmix(z, zg, conv_y, conv_q, xs, l, pool_w, pool_scale, conf_ln_g, conf_ln_b,
                          gate_b, wa, wb, wc, wo, norm2_g)
        act, wd = _ffn_up(h, ffn_up, ffn_conv_w, ffn_down, l)
        last = l == DEPTH - 1
        g_next = final_g if last else norm1_g[l + 1]
        outs = _ffn_down(act, wd, xs, g_next, last)
        if last:
            (y,) = outs
        else:
            xs, h = outs
    return y.reshape(BATCH, SEQ, D_MODEL)
```

```python
import functools

import jax
import jax.numpy as jnp
from jax import lax
from jax.experimental import pallas as pl
from jax.experimental.pallas import tpu as pltpu

D_MODEL = 2048
BATCH = 4
SEQ = 2048
DEPTH = 2
ROWS = BATCH * SEQ

POOL_WIDTH = 1024
POOL_GROUPS = 4
POOL_WINDOWS = (2, 4, 8, 16)
POOL_GROUP_DIM = POOL_WIDTH // POOL_GROUPS
CONF_WIDTH = 1024
CONF_KERNEL = 31
SCONV_WIDTH = 1024
SCONV_KERNEL = 3
N_BRANCH = 3
BRANCH_WIDTH = 1024
OFF_POOL = POOL_WIDTH
OFF_CONF = OFF_POOL + 2 * CONF_WIDTH
OFF_SCONV = OFF_CONF + 3 * SCONV_WIDTH
GATE_WIDTH = N_BRANCH * D_MODEL
D_IN = OFF_SCONV + GATE_WIDTH
D_FF = 5504
FFN_KERNEL = 3
EPS = 1e-6

V7X_VMEM_BYTES = 64 * 1024 * 1024
SUBLANES = 8
LANES = 128

BF16 = jnp.bfloat16
F32 = jnp.float32


def _compiler_params(semantics, vmem_estimate_bytes):
    limit = min(int(vmem_estimate_bytes * 1.25) + (4 << 20), V7X_VMEM_BYTES - (4 << 20))
    return pltpu.CompilerParams(dimension_semantics=semantics, vmem_limit_bytes=limit)


def _sigmoid(x):
    return 1.0 / (1.0 + jnp.exp(-x))


def _rms_scale(x, g):
    ms = jnp.mean(x * x, axis=-1, keepdims=True)
    return x * lax.rsqrt(ms + EPS) * g


def _layer_row(stacked):
    return stacked.reshape(stacked.shape[0], 1, stacked.shape[1])


def _layer_spec(shape, layer, grid_rank, **kwargs):
    zeros = (0,) * len(shape)
    if grid_rank == 1:
        index_map = lambda i: (layer,) + zeros
    else:
        index_map = lambda j, i: (layer,) + zeros
    return pl.BlockSpec((None,) + tuple(shape), index_map, **kwargs)


NORM_TM = 512


def _rmsnorm_kernel(x_ref, g_ref, o_ref):
    o_ref[...] = _rms_scale(x_ref[...], g_ref[...]).astype(o_ref.dtype)


def _rmsnorm_cast(x, g, layer):
    tm = NORM_TM
    est = 2 * tm * D_MODEL * (4 + 2) + 4 * tm * D_MODEL * 4
    return pl.pallas_call(
        _rmsnorm_kernel,
        grid=(ROWS // tm,),
        in_specs=[pl.BlockSpec((tm, D_MODEL), lambda i: (i, 0)),
                  _layer_spec((1, D_MODEL), layer, 1)],
        out_specs=pl.BlockSpec((tm, D_MODEL), lambda i: (i, 0)),
        out_shape=jax.ShapeDtypeStruct((ROWS, D_MODEL), BF16),
        compiler_params=_compiler_params(("arbitrary",), est),
        name="rmsnorm_cast",
    )(x, _layer_row(g))


CONV_HALO = 32
CONV_ROW_BLOCK = 256


def _shifted_causal_conv(buf_ref, buf_cols, shf_ref, cw_ref, bias, cols, taps, tm, out_ref):
    base = CONV_HALO - (taps - 1)
    span = CONV_HALO + tm - SUBLANES
    residues = sorted({(base + k) % SUBLANES for k in range(taps)} - {0})
    for r in residues:
        shf_ref[r - 1, :, :] = buf_ref[r:r + span, buf_cols]
    for lo in range(0, tm, CONV_ROW_BLOCK):
        acc = bias
        for k in range(taps):
            a, r = divmod(base + k, SUBLANES)
            row = a * SUBLANES + lo
            if r == 0:
                src = buf_ref[row:row + CONV_ROW_BLOCK, buf_cols]
            else:
                src = shf_ref[r - 1, row:row + CONV_ROW_BLOCK, :]
            term = cw_ref[k:k + 1, cols] * src
            acc = term if acc is None else acc + term
        out_ref[lo:lo + CONV_ROW_BLOCK, cols] = acc.astype(out_ref.dtype)


INPROJ_TM = 1024
INPROJ_TN = 1024
INPROJ_HALF = INPROJ_TN // 2
INPROJ_ROW_TILES = ROWS // INPROJ_TM
INPROJ_TILES_PER_SEQ = SEQ // INPROJ_TM
MIX_TILES = CONF_WIDTH // LANES
MIX_GATE_WIDTH = MIX_TILES * INPROJ_HALF
TILE_POOL = 0
TILE_MIX = TILE_POOL + POOL_WIDTH // INPROJ_TN
TILE_BG = TILE_MIX + MIX_TILES
TILE_GATE = TILE_BG + SCONV_WIDTH // INPROJ_TN
GATE_TILES = (GATE_WIDTH - MIX_GATE_WIDTH) // INPROJ_TN
INPROJ_COL_TILES = TILE_GATE + GATE_TILES
Z_POOL_BLOCK = GATE_TILES
Z_BG_BLOCK = Z_POOL_BLOCK + POOL_WIDTH // INPROJ_TN
Z_WIDTH = GATE_WIDTH - MIX_GATE_WIDTH + POOL_WIDTH + SCONV_WIDTH
CAST_ROWS = 256


def _cast_weight_tile(w_ref, wbf_ref, lane_shift=0):
    rows, cols = wbf_ref.shape
    keep = cols - lane_shift

    def body(r, carry):
        sl = pl.ds(pl.multiple_of(r * CAST_ROWS, CAST_ROWS), CAST_ROWS)
        wbf_ref[sl, 0:keep] = w_ref[sl, lane_shift:cols].astype(BF16)
        if lane_shift:
            wbf_ref[sl, keep:cols] = jnp.zeros((CAST_ROWS, lane_shift), BF16)
        return carry

    lax.fori_loop(0, rows // CAST_ROWS, body, 0)


def _inproj_kernel(h_ref, w0_ref, w1_ref, w2_ref, w3_ref, w4_ref, cw_ref, cb_ref, sw_ref,
                   z_ref, zg_ref, y_ref, q_ref, wbf_ref, v_ref, shf_ref, g_ref):
    tm, half = INPROJ_TM, INPROJ_HALF
    j, i = pl.program_id(0), pl.program_id(1)
    is_mix = (j >= TILE_MIX) & (j < TILE_MIX + MIX_TILES)

    @pl.when(i == 0)
    def _():
        for k, w_ref in enumerate((w0_ref, w1_ref, w2_ref, w3_ref)):
            _cast_weight_tile(w_ref, wbf_ref.at[:, k * LANES:(k + 1) * LANES])
        _cast_weight_tile(w4_ref, wbf_ref.at[:, half:2 * half])

    @pl.when(jnp.logical_not(is_mix))
    def _():
        z_ref[...] = jnp.dot(h_ref[...], wbf_ref[...],
                             preferred_element_type=F32).astype(z_ref.dtype)

    @pl.when(is_mix)
    def _():
        @pl.when(i % INPROJ_TILES_PER_SEQ == 0)
        def _():
            v_ref[0:CONV_HALO, :] = jnp.zeros((CONV_HALO, 2 * LANES), F32)

        h = h_ref[...]
        cols = slice(0, LANES)
        pair = jnp.dot(h, wbf_ref[:, 0:2 * LANES], preferred_element_type=F32)
        v_ref[CONV_HALO:, cols] = pair[:, 0:LANES] * _sigmoid(pair[:, LANES:2 * LANES])
        _shifted_causal_conv(v_ref, cols, shf_ref, cw_ref, cb_ref[...], cols, CONF_KERNEL, tm,
                             y_ref)
        for c in range(half // (2 * LANES)):
            gcols = slice(c * 2 * LANES, (c + 1) * 2 * LANES)
            g_ref[:, gcols] = jnp.dot(h, wbf_ref[:, half + c * 2 * LANES:half + (c + 1) * 2 * LANES],
                                      preferred_element_type=F32)
        pair = jnp.dot(h, wbf_ref[:, 2 * LANES:4 * LANES], preferred_element_type=F32)
        vcols = slice(LANES, 2 * LANES)
        v_ref[CONV_HALO:, vcols] = pair[:, 0:LANES] * pair[:, LANES:2 * LANES]
        _shifted_causal_conv(v_ref, vcols, shf_ref, sw_ref, None, cols, SCONV_KERNEL, tm, q_ref)
        zg_ref[...] = g_ref[...].astype(zg_ref.dtype)
        v_ref[0:CONV_HALO, :] = v_ref[tm:tm + CONV_HALO, :]


def _inproj(h, w_in, conf_w, conf_b, sconv_w, layer):
    tm, tn, half = INPROJ_TM, INPROJ_TN, INPROJ_HALF
    last_row = INPROJ_ROW_TILES - 1
    mix = lambda j: jnp.clip(j - TILE_MIX, 0, MIX_TILES - 1)
    is_mix = lambda j: (j >= TILE_MIX) & (j < TILE_MIX + MIX_TILES)
    mixed_col = (OFF_POOL, OFF_POOL + CONF_WIDTH, OFF_CONF + SCONV_WIDTH, OFF_CONF + 2 * SCONV_WIDTH)

    def plain_col(j):
        return jnp.where(j < TILE_MIX, tn * (j - TILE_POOL),
                         jnp.where(j < TILE_GATE, OFF_CONF + tn * (j - TILE_BG),
                                   OFF_SCONV + MIX_GATE_WIDTH + tn * (j - TILE_GATE)))

    def w128_spec(k):
        return pl.BlockSpec(
            (None, D_MODEL, LANES),
            lambda j, i: (layer, 0, jnp.where(is_mix(j), mixed_col[k] // LANES + mix(j),
                                             plain_col(j) // LANES + k)))

    w512_spec = pl.BlockSpec(
        (None, D_MODEL, half),
        lambda j, i: (layer, 0, jnp.where(is_mix(j), OFF_SCONV // half + mix(j),
                                         plain_col(j) // half + 1)))

    def z_index(j, i):
        row = jnp.where(is_mix(j), last_row, i)
        col = jnp.where(j < TILE_BG, Z_POOL_BLOCK,
                        jnp.where(j < TILE_GATE, Z_BG_BLOCK, j - TILE_GATE))
        return (row, col)

    def mix_index(j, i):
        row = jnp.where(j < TILE_MIX, 0, jnp.where(j >= TILE_MIX + MIX_TILES, last_row, i))
        return (row, mix(j))

    est = (2 * tm * D_MODEL * 2 + 2 * D_MODEL * tn * 4 + D_MODEL * tn * 2 + 2 * tm * tn * 2
           + 2 * tm * half * 2 + 2 * tm * LANES * (4 + 2) + (CONV_HALO + tm) * 2 * LANES * 4
           + (SUBLANES - 1) * (CONV_HALO + tm) * LANES * 4 + tm * half * 4
           + 4 * tm * 2 * LANES * 4)
    return pl.pallas_call(
        _inproj_kernel,
        grid=(INPROJ_COL_TILES, INPROJ_ROW_TILES),
        in_specs=[pl.BlockSpec((tm, D_MODEL), lambda j, i: (i, 0)),
                  w128_spec(0), w128_spec(1), w128_spec(2), w128_spec(3), w512_spec,
                  pl.BlockSpec((None, CONF_KERNEL, LANES), lambda j, i: (layer, 0, mix(j))),
                  pl.BlockSpec((None, 1, LANES), lambda j, i: (layer, 0, mix(j))),
                  pl.BlockSpec((None, SCONV_KERNEL, LANES), lambda j, i: (layer, 0, mix(j)))],
        out_specs=[pl.BlockSpec((tm, tn), z_index),
                   pl.BlockSpec((tm, half), mix_index),
                   pl.BlockSpec((tm, LANES), mix_index),
                   pl.BlockSpec((tm, LANES), mix_index)],
        out_shape=[jax.ShapeDtypeStruct((ROWS, Z_WIDTH), BF16),
                   jax.ShapeDtypeStruct((ROWS, MIX_GATE_WIDTH), BF16),
                   jax.ShapeDtypeStruct((ROWS, CONF_WIDTH), F32),
                   jax.ShapeDtypeStruct((ROWS, SCONV_WIDTH), BF16)],
        scratch_shapes=[pltpu.VMEM((D_MODEL, tn), BF16),
                        pltpu.VMEM((CONV_HALO + tm, 2 * LANES), F32),
                        pltpu.VMEM((SUBLANES - 1, CONV_HALO + tm - SUBLANES, LANES), F32),
                        pltpu.VMEM((tm, half), F32)],
        compiler_params=_compiler_params(("arbitrary", "arbitrary"), est),
        name="inproj",
    )(h, w_in, w_in, w_in, w_in, w_in, conf_w, _layer_row(conf_b), sconv_w)


TM_TM = 256
TM_HALO = 32
TM_TILES = ROWS // TM_TM
TM_TILES_PER_SEQ = SEQ // TM_TM
MIX_NC = 512


def _tokenmix_kernel(up_ref, upp_ref, bg_ref, q_ref, y_ref, zga_ref, zgb_ref, x_ref, pw_ref, ps_ref,
                     lg_ref, lb_ref, gb_ref, wa_ref, wb_ref, wc_ref, wo_ref, g_ref, xo_ref, ho_ref,
                     fa_ref, fb_ref, fc_ref, buf_ref, mix_ref, pool_ref):
    tm, halo = TM_TM, TM_HALO
    t_idx = pl.program_id(0) % TM_TILES_PER_SEQ
    keep = jnp.where(t_idx == 0, 0.0, 1.0).astype(F32)

    buf_ref[0:halo, :] = upp_ref[...].astype(F32) * keep
    buf_ref[halo:, :] = up_ref[...].astype(F32)
    pos = (t_idx * tm + 1 + lax.broadcasted_iota(jnp.int32, (tm, 1), 0)).astype(F32)
    ext = halo + tm
    pool_ref[0:SUBLANES, :] = jnp.zeros((SUBLANES, POOL_GROUP_DIM), F32)
    for g, w in enumerate(POOL_WINDOWS):
        cols = slice(g * POOL_GROUP_DIM, (g + 1) * POOL_GROUP_DIM)
        ws = buf_ref[:, cols]
        x = ws[halo:, :]
        d = 1
        while d < w:
            pool_ref[d:d + ext, :] = ws
            ws = ws + pool_ref[0:ext, :]
            d *= 2
        inv_count = 1.0 / jnp.minimum(pos, float(w))
        pooled = ws[halo:, :] * inv_count - x
        mixed = jnp.dot(pooled.astype(BF16), pw_ref[g].astype(BF16), preferred_element_type=F32)
        fa_ref[:, cols] = (mixed * ps_ref[:, cols]).astype(fa_ref.dtype)

    fc_ref[...] = (bg_ref[...].astype(F32) * q_ref[...].astype(F32)).astype(fc_ref.dtype)

    y = y_ref[...]
    mu = jnp.mean(y, axis=-1, keepdims=True)
    yc = y - mu
    var = jnp.mean(yc * yc, axis=-1, keepdims=True)
    yn = yc * lax.rsqrt(var + EPS) * lg_ref[...] + lb_ref[...]
    fb_ref[...] = (yn * _sigmoid(yn)).astype(fb_ref.dtype)

    for c in range(D_MODEL // MIX_NC):
        cols = slice(c * MIX_NC, (c + 1) * MIX_NC)
        acc = None
        for k, (f_ref, w_ref) in enumerate(((fa_ref, wa_ref), (fb_ref, wb_ref), (fc_ref, wc_ref))):
            gcols = slice(k * D_MODEL + c * MIX_NC, k * D_MODEL + (c + 1) * MIX_NC)
            if gcols.stop <= MIX_GATE_WIDTH:
                logits = zga_ref[:, gcols]
            else:
                logits = zgb_ref[:, gcols.start - MIX_GATE_WIDTH:gcols.stop - MIX_GATE_WIDTH]
            gate = _sigmoid(logits.astype(F32) + gb_ref[:, gcols])
            term = gate * jnp.dot(f_ref[...], w_ref[:, cols], preferred_element_type=F32)
            acc = term if acc is None else acc + term
        mix_ref[:, cols] = acc.astype(mix_ref.dtype)

    for c in range(D_MODEL // MIX_NC):
        cols = slice(c * MIX_NC, (c + 1) * MIX_NC)
        xo_ref[:, cols] = x_ref[:, cols] + jnp.dot(mix_ref[...], wo_ref[:, cols],
                                                   preferred_element_type=F32)
    ho_ref[...] = _rms_scale(xo_ref[...], g_ref[...]).astype(ho_ref.dtype)


def _tokenmix(z, zg, y, q, x, layer, pool_w, pool_scale, ln_g, ln_b, gate_b, wa, wb, wc, wo, norm_g):
    tm, halo = TM_TM, TM_HALO
    ratio = tm // halo
    halo_tile = lambda s: jnp.maximum(s * ratio - 1, 0)
    lspec = lambda shape, **kw: _layer_spec(shape, layer, 1, **kw)
    once = dict(pipeline_mode=pl.Buffered(1))
    row_spec = pl.BlockSpec((tm, D_MODEL), lambda s: (s, 0))
    feat = pltpu.VMEM((tm, BRANCH_WIDTH), BF16)
    est = (2 * (tm + halo) * POOL_WIDTH * 2 + 2 * 2 * tm * SCONV_WIDTH * 2 + 2 * tm * CONF_WIDTH * 4
           + 2 * tm * GATE_WIDTH * 2 + 2 * tm * D_MODEL * (4 + 4 + 2)
           + POOL_WIDTH * POOL_GROUP_DIM * 4 + (3 * BRANCH_WIDTH + D_MODEL) * D_MODEL * 2
           + (tm + halo) * BRANCH_WIDTH * 4 + 3 * tm * BRANCH_WIDTH * 2 + tm * D_MODEL * 2
           + 8 * tm * MIX_NC * 4)
    return pl.pallas_call(
        _tokenmix_kernel,
        grid=(TM_TILES,),
        in_specs=[pl.BlockSpec((tm, POOL_WIDTH), lambda s: (s, Z_POOL_BLOCK)),
                  pl.BlockSpec((halo, POOL_WIDTH), lambda s: (halo_tile(s), Z_POOL_BLOCK)),
                  pl.BlockSpec((tm, SCONV_WIDTH), lambda s: (s, Z_BG_BLOCK)),
                  pl.BlockSpec((tm, SCONV_WIDTH), lambda s: (s, 0)),
                  pl.BlockSpec((tm, CONF_WIDTH), lambda s: (s, 0)),
                  pl.BlockSpec((tm, MIX_GATE_WIDTH), lambda s: (s, 0)),
                  pl.BlockSpec((tm, GATE_WIDTH - MIX_GATE_WIDTH), lambda s: (s, 0)),
                  row_spec,
                  lspec((POOL_GROUPS, POOL_GROUP_DIM, POOL_GROUP_DIM), **once),
                  lspec((1, POOL_WIDTH)),
                  lspec((1, CONF_WIDTH)), lspec((1, CONF_WIDTH)),
                  lspec((1, GATE_WIDTH)),
                  lspec((BRANCH_WIDTH, D_MODEL), **once), lspec((BRANCH_WIDTH, D_MODEL), **once),
                  lspec((BRANCH_WIDTH, D_MODEL), **once), lspec((D_MODEL, D_MODEL), **once),
                  lspec((1, D_MODEL))],
        out_specs=[row_spec, row_spec],
        out_shape=[jax.ShapeDtypeStruct((ROWS, D_MODEL), F32),
                   jax.ShapeDtypeStruct((ROWS, D_MODEL), BF16)],
        scratch_shapes=[feat, feat, feat,
                        pltpu.VMEM((halo + tm, BRANCH_WIDTH), F32),
                        pltpu.VMEM((tm, D_MODEL), BF16),
                        pltpu.VMEM((halo + tm + max(POOL_WINDOWS) // 2, POOL_GROUP_DIM), F32)],
        compiler_params=_compiler_params(("arbitrary",), est),
        name="tokenmix",
    )(z, z, z, q, y, zg, z, x, pool_w, _layer_row(pool_scale), _layer_row(ln_g), _layer_row(ln_b),
      _layer_row(gate_b), wa, wb, wc, wo, _layer_row(norm_g))


UP_TM = 1024
UP_TN = 512
UP_ROW_TILES = ROWS // UP_TM
UP_TILES_PER_SEQ = SEQ // UP_TM
UP_COL_TILES = pl.cdiv(D_FF, UP_TN)
D_FF_PAD = UP_COL_TILES * UP_TN
UP_LAST_SHIFT = D_FF_PAD - D_FF
UP_TAIL = SUBLANES
DOWN_SLAB = 256
DOWN_SLABS_PER_COL_TILE = UP_TN // DOWN_SLAB
DOWN_SLAB_EVERY = UP_ROW_TILES // DOWN_SLABS_PER_COL_TILE
DOWN_LAST_SLAB_ROWS = D_FF - (D_FF_PAD - DOWN_SLAB)


def _up_col_start(j, offset=0):
    tile = jnp.minimum(j * (UP_TN // LANES), (D_FF - UP_TN) // LANES)
    return (offset // LANES + tile) * LANES


def _down_slab(j, i):
    return j * DOWN_SLABS_PER_COL_TILE + i // DOWN_SLAB_EVERY


def _ffn_up_kernel(h_ref, wg_ref, wv_ref, cg_ref, cv_ref, wd_ref, o_ref, wdo_ref,
                   wg_bf, wv_bf, cw_ref, ug_ref, uv_ref):
    tm, tn = UP_TM, UP_TN
    j, i = pl.program_id(0), pl.program_id(1)
    last = UP_COL_TILES - 1

    def stage_weights(shift):
        keep = tn - shift
        _cast_weight_tile(wg_ref.at[0], wg_bf, shift)
        _cast_weight_tile(wv_ref.at[0], wv_bf, shift)
        for r, ref in enumerate((cg_ref, cv_ref)):
            rows = slice(r * SUBLANES, r * SUBLANES + FFN_KERNEL)
            cw_ref[rows, 0:keep] = ref[0, :, shift:tn]
            if shift:
                cw_ref[rows, keep:tn] = jnp.zeros((FFN_KERNEL, shift), F32)

    @pl.when((i == 0) & (j != last))
    def _():
        stage_weights(0)

    @pl.when((i == 0) & (j == last))
    def _():
        stage_weights(UP_LAST_SHIFT)

    def stage_down(rows):
        wdo_ref[0:rows, :] = wd_ref[0:rows, :].astype(BF16)
        if rows < DOWN_SLAB:
            wdo_ref[rows:DOWN_SLAB, :] = jnp.zeros((DOWN_SLAB - rows, D_MODEL), BF16)

    is_visit = i % DOWN_SLAB_EVERY == 0
    is_last_slab = (j == last) & (i // DOWN_SLAB_EVERY == DOWN_SLABS_PER_COL_TILE - 1)

    @pl.when(is_visit & jnp.logical_not(is_last_slab))
    def _():
        stage_down(DOWN_SLAB)

    @pl.when(is_visit & is_last_slab)
    def _():
        stage_down(DOWN_LAST_SLAB_ROWS)

    @pl.when(i % UP_TILES_PER_SEQ == 0)
    def _():
        for u_ref in (ug_ref, uv_ref):
            u_ref[:, tm:tm + UP_TAIL, :] = jnp.zeros((tn // LANES, UP_TAIL, LANES), F32)

    for u_ref in (ug_ref, uv_ref):
        u_ref[:, 0:UP_TAIL, :] = u_ref[:, tm:tm + UP_TAIL, :]

    h = h_ref[...]
    for u_ref, w_bf in ((ug_ref, wg_bf), (uv_ref, wv_bf)):
        up = jnp.dot(h, w_bf[...], preferred_element_type=F32)
        for c in range(tn // LANES):
            u_ref[c, UP_TAIL:UP_TAIL + tm, :] = up[:, c * LANES:(c + 1) * LANES]

    def conv(u_ref, w_row, c):
        acc = None
        for k in range(FFN_KERNEL):
            first = UP_TAIL - (FFN_KERNEL - 1 - k)
            term = (cw_ref[w_row + k:w_row + k + 1, c * LANES:(c + 1) * LANES]
                    * u_ref[c, first:first + tm, :])
            acc = term if acc is None else acc + term
        return acc

    for c in range(tn // LANES):
        gt = conv(ug_ref, 0, c)
        vl = conv(uv_ref, SUBLANES, c)
        o_ref[:, c * LANES:(c + 1) * LANES] = (gt * _sigmoid(gt) * vl).astype(o_ref.dtype)


def _ffn_up(h, ffn_up, ffn_conv_w, ffn_down, layer):
    tm, tn = UP_TM, UP_TN
    w_spec = lambda off: pl.BlockSpec((pl.Element(1), pl.Element(D_MODEL), pl.Element(tn)),
                                      lambda j, i: (layer, 0, _up_col_start(j, off)))
    cw_spec = lambda off: pl.BlockSpec((pl.Element(1), pl.Element(FFN_KERNEL), pl.Element(tn)),
                                       lambda j, i: (layer, 0, _up_col_start(j, off)))
    u_scratch = pltpu.VMEM((tn // LANES, UP_TAIL + tm, LANES), F32)
    est = (2 * tm * D_MODEL * 2 + 2 * 2 * D_MODEL * tn * 4 + 2 * D_MODEL * tn * 2 + 2 * tm * tn * 2
           + 2 * (tm + UP_TAIL) * tn * 4 + 2 * DOWN_SLAB * D_MODEL * (4 + 2)
           + 4 * tm * tn * 4)
    return pl.pallas_call(
        _ffn_up_kernel,
        grid=(UP_COL_TILES, UP_ROW_TILES),
        in_specs=[pl.BlockSpec((tm, D_MODEL), lambda j, i: (i, 0)),
                  w_spec(0), w_spec(D_FF), cw_spec(0), cw_spec(D_FF),
                  pl.BlockSpec((None, DOWN_SLAB, D_MODEL),
                               lambda j, i: (layer, _down_slab(j, i), 0))],
        out_specs=[pl.BlockSpec((tm, tn), lambda j, i: (i, j)),
                   pl.BlockSpec((DOWN_SLAB, D_MODEL), lambda j, i: (_down_slab(j, i), 0))],
        out_shape=[jax.ShapeDtypeStruct((ROWS, D_FF_PAD), BF16),
                   jax.ShapeDtypeStruct((D_FF_PAD, D_MODEL), BF16)],
        scratch_shapes=[pltpu.VMEM((D_MODEL, tn), BF16), pltpu.VMEM((D_MODEL, tn), BF16),
                        pltpu.VMEM((2 * SUBLANES, tn), F32), u_scratch, u_scratch],
        compiler_params=_compiler_params(("arbitrary", "arbitrary"), est),
        name="ffn_up",
    )(h, ffn_up, ffn_up, ffn_conv_w, ffn_conv_w, ffn_down)


DOWN_TM = 512


def _ffn_down_kernel(act_ref, w_ref, x_ref, g_ref, *out_refs, last):
    xn = x_ref[...] + jnp.dot(act_ref[...], w_ref[...], preferred_element_type=F32)
    normed = _rms_scale(xn, g_ref[...])
    if last:
        (y_ref,) = out_refs
        y_ref[...] = normed
    else:
        xo_ref, ho_ref = out_refs
        xo_ref[...] = xn
        ho_ref[...] = normed.astype(ho_ref.dtype)


def _ffn_down(act, wd, x, g, last):
    tm = DOWN_TM
    row_spec = pl.BlockSpec((tm, D_MODEL), lambda i: (i, 0))
    if last:
        out_specs = [row_spec]
        out_shape = [jax.ShapeDtypeStruct((ROWS, D_MODEL), F32)]
    else:
        out_specs = [row_spec, row_spec]
        out_shape = [jax.ShapeDtypeStruct((ROWS, D_MODEL), F32),
                     jax.ShapeDtypeStruct((ROWS, D_MODEL), BF16)]
    est = (2 * tm * D_FF_PAD * 2 + D_FF_PAD * D_MODEL * 2 + 2 * tm * D_MODEL * (4 + 4 + 2)
           + 3 * tm * D_MODEL * 4)
    return pl.pallas_call(
        functools.partial(_ffn_down_kernel, last=last),
        grid=(ROWS // tm,),
        in_specs=[pl.BlockSpec((tm, D_FF_PAD), lambda i: (i, 0)),
                  pl.BlockSpec((D_FF_PAD, D_MODEL), lambda i: (0, 0), pipeline_mode=pl.Buffered(1)),
                  row_spec,
                  pl.BlockSpec((1, D_MODEL), lambda i: (0, 0))],
        out_specs=out_specs,
        out_shape=out_shape,
        compiler_params=_compiler_params(("arbitrary",), est),
        name="ffn_down",
    )(act, wd, x, g.reshape(1, -1))


def kernel(x, norm1_g, w_in, gate_b, pool_w, pool_scale, pool_proj, conf_conv_w, conf_conv_b,
           conf_ln_g, conf_ln_b, conf_proj, sconv_w, sconv_proj, w_o, norm2_g, ffn_up, ffn_conv_w,
           ffn_down, final_g):
    assert x.shape == (BATCH, SEQ, D_MODEL) and w_in.shape == (DEPTH, D_MODEL, D_IN)
    xs = x.reshape(ROWS, D_MODEL)
    wa, wb, wc, wo = (w.astype(BF16) for w in (pool_proj, conf_proj, sconv_proj, w_o))
    h = _rmsnorm_cast(xs, norm1_g, 0)
    for l in range(DEPTH):
        z, zg, conv_y, conv_q = _inproj(h, w_in, conf_conv_w, conf_conv_b, sconv_w, l)
        xs, h = _tokenmix(z, zg, conv_y, conv_q, xs, l, pool_w, pool_scale, conf_ln_g, conf_ln_b,
                          gate_b, wa, wb, wc, wo, norm2_g)
        act, wd = _ffn_up(h, ffn_up, ffn_conv_w, ffn_down, l)
        last = l == DEPTH - 1
        g_next = final_g if last else norm1_g[l + 1]
        outs = _ffn_down(act, wd, xs, g_next, last)
        if last:
            (y,) = outs
        else:
            xs, h = outs
    return y.reshape(BATCH, SEQ, D_MODEL)
```

```python
import functools

import jax
import jax.numpy as jnp
from jax import lax
from jax.experimental import pallas as pl
from jax.experimental.pallas import tpu as pltpu

D_MODEL = 2048
BATCH = 4
SEQ = 2048
DEPTH = 2
ROWS = BATCH * SEQ

POOL_WIDTH = 1024
POOL_GROUPS = 4
POOL_WINDOWS = (2, 4, 8, 16)
POOL_GROUP_DIM = POOL_WIDTH // POOL_GROUPS
CONF_WIDTH = 1024
CONF_KERNEL = 31
SCONV_WIDTH = 1024
SCONV_KERNEL = 3
N_BRANCH = 3
BRANCH_WIDTH = 1024
OFF_POOL = POOL_WIDTH
OFF_CONF = OFF_POOL + 2 * CONF_WIDTH
OFF_SCONV = OFF_CONF + 3 * SCONV_WIDTH
GATE_WIDTH = N_BRANCH * D_MODEL
D_IN = OFF_SCONV + GATE_WIDTH
D_FF = 5504
FFN_KERNEL = 3
EPS = 1e-6

V7X_VMEM_BYTES = 64 * 1024 * 1024
SUBLANES = 8
LANES = 128

BF16 = jnp.bfloat16
F32 = jnp.float32


def _compiler_params(semantics, vmem_estimate_bytes):
    limit = min(int(vmem_estimate_bytes * 1.25) + (4 << 20), V7X_VMEM_BYTES - (4 << 20))
    return pltpu.CompilerParams(dimension_semantics=semantics, vmem_limit_bytes=limit)


def _sigmoid(x):
    return 1.0 / (1.0 + jnp.exp(-x))


def _rms_scale(x, g):
    ms = jnp.mean(x * x, axis=-1, keepdims=True)
    return x * lax.rsqrt(ms + EPS) * g


def _layer_row(stacked):
    return stacked.reshape(stacked.shape[0], 1, stacked.shape[1])


def _layer_spec(shape, layer, grid_rank, **kwargs):
    zeros = (0,) * len(shape)
    if grid_rank == 1:
        index_map = lambda i: (layer,) + zeros
    else:
        index_map = lambda j, i: (layer,) + zeros
    return pl.BlockSpec((None,) + tuple(shape), index_map, **kwargs)


NORM_TM = 512


def _rmsnorm_kernel(x_ref, g_ref, o_ref):
    o_ref[...] = _rms_scale(x_ref[...], g_ref[...]).astype(o_ref.dtype)


def _rmsnorm_cast(x, g, layer):
    tm = NORM_TM
    est = 2 * tm * D_MODEL * (4 + 2) + 4 * tm * D_MODEL * 4
    return pl.pallas_call(
        _rmsnorm_kernel,
        grid=(ROWS // tm,),
        in_specs=[pl.BlockSpec((tm, D_MODEL), lambda i: (i, 0)),
                  _layer_spec((1, D_MODEL), layer, 1)],
        out_specs=pl.BlockSpec((tm, D_MODEL), lambda i: (i, 0)),
        out_shape=jax.ShapeDtypeStruct((ROWS, D_MODEL), BF16),
        compiler_params=_compiler_params(("arbitrary",), est),
        name="rmsnorm_cast",
    )(x, _layer_row(g))


CONV_HALO = 32
CONV_ROW_BLOCK = 256


def _shifted_causal_conv(buf_ref, buf_cols, shf_ref, cw_ref, bias, cols, taps, tm, out_ref):
    base = CONV_HALO - (taps - 1)
    span = CONV_HALO + tm - SUBLANES
    residues = sorted({(base + k) % SUBLANES for k in range(taps)} - {0})
    for r in residues:
        shf_ref[r - 1, :, :] = buf_ref[r:r + span, buf_cols]
    for lo in range(0, tm, CONV_ROW_BLOCK):
        acc = bias
        for k in range(taps):
            a, r = divmod(base + k, SUBLANES)
            row = a * SUBLANES + lo
            if r == 0:
                src = buf_ref[row:row + CONV_ROW_BLOCK, buf_cols]
            else:
                src = shf_ref[r - 1, row:row + CONV_ROW_BLOCK, :]
            term = cw_ref[k:k + 1, cols] * src
            acc = term if acc is None else acc + term
        out_ref[lo:lo + CONV_ROW_BLOCK, cols] = acc.astype(out_ref.dtype)


INPROJ_TM = 1024
INPROJ_TN = 1024
INPROJ_HALF = INPROJ_TN // 2
INPROJ_ROW_TILES = ROWS // INPROJ_TM
INPROJ_TILES_PER_SEQ = SEQ // INPROJ_TM
MIX_TILES = CONF_WIDTH // LANES
MIX_GATE_WIDTH = MIX_TILES * INPROJ_HALF
TILE_POOL = 0
TILE_MIX = TILE_POOL + POOL_WIDTH // INPROJ_TN
TILE_BG = TILE_MIX + MIX_TILES
TILE_GATE = TILE_BG + SCONV_WIDTH // INPROJ_TN
GATE_TILES = (GATE_WIDTH - MIX_GATE_WIDTH) // INPROJ_TN
INPROJ_COL_TILES = TILE_GATE + GATE_TILES
Z_POOL_BLOCK = GATE_TILES
Z_BG_BLOCK = Z_POOL_BLOCK + POOL_WIDTH // INPROJ_TN
Z_WIDTH = GATE_WIDTH - MIX_GATE_WIDTH + POOL_WIDTH + SCONV_WIDTH
CAST_ROWS = 256
H_SLOTS = 3


def _cast_weight_tile(w_ref, wbf_ref, lane_shift=0):
    rows, cols = wbf_ref.shape
    keep = cols - lane_shift

    def body(r, carry):
        sl = pl.ds(pl.multiple_of(r * CAST_ROWS, CAST_ROWS), CAST_ROWS)
        wbf_ref[sl, 0:keep] = w_ref[sl, lane_shift:cols].astype(BF16)
        if lane_shift:
            wbf_ref[sl, keep:cols] = jnp.zeros((CAST_ROWS, lane_shift), BF16)
        return carry

    lax.fori_loop(0, rows // CAST_ROWS, body, 0)


def _inproj_kernel(h_hbm, w0_ref, w1_ref, w2_ref, w3_ref, w4_ref, cw_ref, cb_ref, sw_ref,
                   z_ref, zg_ref, y_ref, q_ref, wbf_ref, v_ref, shf_ref, g_ref, hbuf_ref, hsem):
    tm, half = INPROJ_TM, INPROJ_HALF
    j, i = pl.program_id(0), pl.program_id(1)
    is_mix = (j >= TILE_MIX) & (j < TILE_MIX + MIX_TILES)

    step = j * INPROJ_ROW_TILES + i

    def h_copy(at_step):
        rows = pl.ds(pl.multiple_of((at_step % INPROJ_ROW_TILES) * tm, tm), tm)
        slot = at_step % H_SLOTS
        return pltpu.make_async_copy(h_hbm.at[rows, :], hbuf_ref.at[slot], hsem.at[slot])

    @pl.when(step == 0)
    def _():
        for first in range(H_SLOTS - 1):
            h_copy(first).start()

    @pl.when(step + H_SLOTS - 1 < INPROJ_COL_TILES * INPROJ_ROW_TILES)
    def _():
        h_copy(step + H_SLOTS - 1).start()

    h_copy(step).wait()
    h_ref = hbuf_ref.at[step % H_SLOTS]

    @pl.when(i == 0)
    def _():
        for k, w_ref in enumerate((w0_ref, w1_ref, w2_ref, w3_ref)):
            _cast_weight_tile(w_ref, wbf_ref.at[:, k * LANES:(k + 1) * LANES])
        _cast_weight_tile(w4_ref, wbf_ref.at[:, half:2 * half])

    @pl.when(jnp.logical_not(is_mix))
    def _():
        z_ref[...] = jnp.dot(h_ref[...], wbf_ref[...],
                             preferred_element_type=F32).astype(z_ref.dtype)

    @pl.when(is_mix)
    def _():
        @pl.when(i % INPROJ_TILES_PER_SEQ == 0)
        def _():
            v_ref[0:CONV_HALO, :] = jnp.zeros((CONV_HALO, 2 * LANES), F32)

        h = h_ref[...]
        cols = slice(0, LANES)
        pair = jnp.dot(h, wbf_ref[:, 0:2 * LANES], preferred_element_type=F32)
        v_ref[CONV_HALO:, cols] = pair[:, 0:LANES] * _sigmoid(pair[:, LANES:2 * LANES])
        _shifted_causal_conv(v_ref, cols, shf_ref, cw_ref, cb_ref[...], cols, CONF_KERNEL, tm,
                             y_ref)
        for c in range(half // (2 * LANES)):
            gcols = slice(c * 2 * LANES, (c + 1) * 2 * LANES)
            g_ref[:, gcols] = jnp.dot(h, wbf_ref[:, half + c * 2 * LANES:half + (c + 1) * 2 * LANES],
                                      preferred_element_type=F32)
        pair = jnp.dot(h, wbf_ref[:, 2 * LANES:4 * LANES], preferred_element_type=F32)
        vcols = slice(LANES, 2 * LANES)
        v_ref[CONV_HALO:, vcols] = pair[:, 0:LANES] * pair[:, LANES:2 * LANES]
        _shifted_causal_conv(v_ref, vcols, shf_ref, sw_ref, None, cols, SCONV_KERNEL, tm, q_ref)
        zg_ref[...] = g_ref[...].astype(zg_ref.dtype)
        v_ref[0:CONV_HALO, :] = v_ref[tm:tm + CONV_HALO, :]


def _inproj(h, w_in, conf_w, conf_b, sconv_w, layer):
    tm, tn, half = INPROJ_TM, INPROJ_TN, INPROJ_HALF
    last_row = INPROJ_ROW_TILES - 1
    mix = lambda j: jnp.clip(j - TILE_MIX, 0, MIX_TILES - 1)
    is_mix = lambda j: (j >= TILE_MIX) & (j < TILE_MIX + MIX_TILES)
    mixed_col = (OFF_POOL, OFF_POOL + CONF_WIDTH, OFF_CONF + SCONV_WIDTH, OFF_CONF + 2 * SCONV_WIDTH)

    def plain_col(j):
        return jnp.where(j < TILE_MIX, tn * (j - TILE_POOL),
                         jnp.where(j < TILE_GATE, OFF_CONF + tn * (j - TILE_BG),
                                   OFF_SCONV + MIX_GATE_WIDTH + tn * (j - TILE_GATE)))

    def w128_spec(k):
        return pl.BlockSpec(
            (None, D_MODEL, LANES),
            lambda j, i: (layer, 0, jnp.where(is_mix(j), mixed_col[k] // LANES + mix(j),
                                             plain_col(j) // LANES + k)))

    w512_spec = pl.BlockSpec(
        (None, D_MODEL, half),
        lambda j, i: (layer, 0, jnp.where(is_mix(j), OFF_SCONV // half + mix(j),
                                         plain_col(j) // half + 1)))

    def z_index(j, i):
        row = jnp.where(is_mix(j), last_row, i)
        col = jnp.where(j < TILE_BG, Z_POOL_BLOCK,
                        jnp.where(j < TILE_GATE, Z_BG_BLOCK, j - TILE_GATE))
        return (row, col)

    def mix_index(j, i):
        row = jnp.where(j < TILE_MIX, 0, jnp.where(j >= TILE_MIX + MIX_TILES, last_row, i))
        return (row, mix(j))

    est = (H_SLOTS * tm * D_MODEL * 2 + 2 * D_MODEL * tn * 4 + D_MODEL * tn * 2 + 2 * tm * tn * 2
           + 2 * tm * half * 2 + 2 * tm * LANES * (4 + 2) + (CONV_HALO + tm) * 2 * LANES * 4
           + (SUBLANES - 1) * (CONV_HALO + tm) * LANES * 4 + tm * half * 4
           + 4 * tm * 2 * LANES * 4)
    return pl.pallas_call(
        _inproj_kernel,
        grid=(INPROJ_COL_TILES, INPROJ_ROW_TILES),
        in_specs=[pl.BlockSpec(memory_space=pl.ANY),
                  w128_spec(0), w128_spec(1), w128_spec(2), w128_spec(3), w512_spec,
                  pl.BlockSpec((None, CONF_KERNEL, LANES), lambda j, i: (layer, 0, mix(j))),
                  pl.BlockSpec((None, 1, LANES), lambda j, i: (layer, 0, mix(j))),
                  pl.BlockSpec((None, SCONV_KERNEL, LANES), lambda j, i: (layer, 0, mix(j)))],
        out_specs=[pl.BlockSpec((tm, tn), z_index),
                   pl.BlockSpec((tm, half), mix_index),
                   pl.BlockSpec((tm, LANES), mix_index),
                   pl.BlockSpec((tm, LANES), mix_index)],
        out_shape=[jax.ShapeDtypeStruct((ROWS, Z_WIDTH), BF16),
                   jax.ShapeDtypeStruct((ROWS, MIX_GATE_WIDTH), BF16),
                   jax.ShapeDtypeStruct((ROWS, CONF_WIDTH), F32),
                   jax.ShapeDtypeStruct((ROWS, SCONV_WIDTH), BF16)],
        scratch_shapes=[pltpu.VMEM((D_MODEL, tn), BF16),
                        pltpu.VMEM((CONV_HALO + tm, 2 * LANES), F32),
                        pltpu.VMEM((SUBLANES - 1, CONV_HALO + tm - SUBLANES, LANES), F32),
                        pltpu.VMEM((tm, half), F32),
                        pltpu.VMEM((H_SLOTS, tm, D_MODEL), BF16),
                        pltpu.SemaphoreType.DMA((H_SLOTS,))],
        compiler_params=_compiler_params(("arbitrary", "arbitrary"), est),
        name="inproj",
    )(h, w_in, w_in, w_in, w_in, w_in, conf_w, _layer_row(conf_b), sconv_w)


TM_TM = 256
TM_HALO = 32
TM_TILES = ROWS // TM_TM
TM_TILES_PER_SEQ = SEQ // TM_TM
MIX_NC = 512


def _tokenmix_kernel(up_ref, upp_ref, bg_ref, q_ref, y_ref, zga_ref, zgb_ref, x_ref, pw_ref, ps_ref,
                     lg_ref, lb_ref, gb_ref, wa_ref, wb_ref, wc_ref, wo_ref, g_ref, xo_ref, ho_ref,
                     fa_ref, fb_ref, fc_ref, buf_ref, mix_ref, pool_ref):
    tm, halo = TM_TM, TM_HALO
    t_idx = pl.program_id(0) % TM_TILES_PER_SEQ
    keep = jnp.where(t_idx == 0, 0.0, 1.0).astype(F32)

    buf_ref[0:halo, :] = upp_ref[...].astype(F32) * keep
    buf_ref[halo:, :] = up_ref[...].astype(F32)
    pos = (t_idx * tm + 1 + lax.broadcasted_iota(jnp.int32, (tm, 1), 0)).astype(F32)
    ext = halo + tm
    pool_ref[0:SUBLANES, :] = jnp.zeros((SUBLANES, POOL_GROUP_DIM), F32)
    for g, w in enumerate(POOL_WINDOWS):
        cols = slice(g * POOL_GROUP_DIM, (g + 1) * POOL_GROUP_DIM)
        ws = buf_ref[:, cols]
        x = ws[halo:, :]
        d = 1
        while d < w:
            pool_ref[d:d + ext, :] = ws
            ws = ws + pool_ref[0:ext, :]
            d *= 2
        inv_count = 1.0 / jnp.minimum(pos, float(w))
        pooled = ws[halo:, :] * inv_count - x
        mixed = jnp.dot(pooled.astype(BF16), pw_ref[g].astype(BF16), preferred_element_type=F32)
        fa_ref[:, cols] = (mixed * ps_ref[:, cols]).astype(fa_ref.dtype)

    fc_ref[...] = (bg_ref[...].astype(F32) * q_ref[...].astype(F32)).astype(fc_ref.dtype)

    y = y_ref[...]
    mu = jnp.mean(y, axis=-1, keepdims=True)
    yc = y - mu
    var = jnp.mean(yc * yc, axis=-1, keepdims=True)
    yn = yc * lax.rsqrt(var + EPS) * lg_ref[...] + lb_ref[...]
    fb_ref[...] = (yn * _sigmoid(yn)).astype(fb_ref.dtype)

    for c in range(D_MODEL // MIX_NC):
        cols = slice(c * MIX_NC, (c + 1) * MIX_NC)
        acc = None
        for k, (f_ref, w_ref) in enumerate(((fa_ref, wa_ref), (fb_ref, wb_ref), (fc_ref, wc_ref))):
            gcols = slice(k * D_MODEL + c * MIX_NC, k * D_MODEL + (c + 1) * MIX_NC)
            if gcols.stop <= MIX_GATE_WIDTH:
                logits = zga_ref[:, gcols]
            else:
                logits = zgb_ref[:, gcols.start - MIX_GATE_WIDTH:gcols.stop - MIX_GATE_WIDTH]
            gate = _sigmoid(logits.astype(F32) + gb_ref[:, gcols])
            term = gate * jnp.dot(f_ref[...], w_ref[:, cols], preferred_element_type=F32)
            acc = term if acc is None else acc + term
        mix_ref[:, cols] = acc.astype(mix_ref.dtype)

    for c in range(D_MODEL // MIX_NC):
        cols = slice(c * MIX_NC, (c + 1) * MIX_NC)
        xo_ref[:, cols] = x_ref[:, cols] + jnp.dot(mix_ref[...], wo_ref[:, cols],
                                                   preferred_element_type=F32)
    ho_ref[...] = _rms_scale(xo_ref[...], g_ref[...]).astype(ho_ref.dtype)


def _tokenmix(z, zg, y, q, x, layer, pool_w, pool_scale, ln_g, ln_b, gate_b, wa, wb, wc, wo, norm_g):
    tm, halo = TM_TM, TM_HALO
    ratio = tm // halo
    halo_tile = lambda s: jnp.maximum(s * ratio - 1, 0)
    lspec = lambda shape, **kw: _layer_spec(shape, layer, 1, **kw)
    once = dict(pipeline_mode=pl.Buffered(1))
    row_spec = pl.BlockSpec((tm, D_MODEL), lambda s: (s, 0))
    feat = pltpu.VMEM((tm, BRANCH_WIDTH), BF16)
    est = (2 * (tm + halo) * POOL_WIDTH * 2 + 2 * 2 * tm * SCONV_WIDTH * 2 + 2 * tm * CONF_WIDTH * 4
           + 2 * tm * GATE_WIDTH * 2 + 2 * tm * D_MODEL * (4 + 4 + 2)
           + POOL_WIDTH * POOL_GROUP_DIM * 4 + (3 * BRANCH_WIDTH + D_MODEL) * D_MODEL * 2
           + (tm + halo) * BRANCH_WIDTH * 4 + 3 * tm * BRANCH_WIDTH * 2 + tm * D_MODEL * 2
           + 8 * tm * MIX_NC * 4)
    return pl.pallas_call(
        _tokenmix_kernel,
        grid=(TM_TILES,),
        in_specs=[pl.BlockSpec((tm, POOL_WIDTH), lambda s: (s, Z_POOL_BLOCK)),
                  pl.BlockSpec((halo, POOL_WIDTH), lambda s: (halo_tile(s), Z_POOL_BLOCK)),
                  pl.BlockSpec((tm, SCONV_WIDTH), lambda s: (s, Z_BG_BLOCK)),
                  pl.BlockSpec((tm, SCONV_WIDTH), lambda s: (s, 0)),
                  pl.BlockSpec((tm, CONF_WIDTH), lambda s: (s, 0)),
                  pl.BlockSpec((tm, MIX_GATE_WIDTH), lambda s: (s, 0)),
                  pl.BlockSpec((tm, GATE_WIDTH - MIX_GATE_WIDTH), lambda s: (s, 0)),
                  row_spec,
                  lspec((POOL_GROUPS, POOL_GROUP_DIM, POOL_GROUP_DIM), **once),
                  lspec((1, POOL_WIDTH)),
                  lspec((1, CONF_WIDTH)), lspec((1, CONF_WIDTH)),
                  lspec((1, GATE_WIDTH)),
                  lspec((BRANCH_WIDTH, D_MODEL), **once), lspec((BRANCH_WIDTH, D_MODEL), **once),
                  lspec((BRANCH_WIDTH, D_MODEL), **once), lspec((D_MODEL, D_MODEL), **once),
                  lspec((1, D_MODEL))],
        out_specs=[row_spec, row_spec],
        out_shape=[jax.ShapeDtypeStruct((ROWS, D_MODEL), F32),
                   jax.ShapeDtypeStruct((ROWS, D_MODEL), BF16)],
        scratch_shapes=[feat, feat, feat,
                        pltpu.VMEM((halo + tm, BRANCH_WIDTH), F32),
                        pltpu.VMEM((tm, D_MODEL), BF16),
                        pltpu.VMEM((halo + tm + max(POOL_WINDOWS) // 2, POOL_GROUP_DIM), F32)],
        compiler_params=_compiler_params(("arbitrary",), est),
        name="tokenmix",
    )(z, z, z, q, y, zg, z, x, pool_w, _layer_row(pool_scale), _layer_row(ln_g), _layer_row(ln_b),
      _layer_row(gate_b), wa, wb, wc, wo, _layer_row(norm_g))


UP_TM = 1024
UP_TN = 512
UP_ROW_TILES = ROWS // UP_TM
UP_TILES_PER_SEQ = SEQ // UP_TM
UP_COL_TILES = pl.cdiv(D_FF, UP_TN)
D_FF_PAD = UP_COL_TILES * UP_TN
UP_LAST_SHIFT = D_FF_PAD - D_FF
UP_TAIL = SUBLANES
DOWN_SLAB = 256
DOWN_SLABS_PER_COL_TILE = UP_TN // DOWN_SLAB
DOWN_SLAB_EVERY = UP_ROW_TILES // DOWN_SLABS_PER_COL_TILE
DOWN_LAST_SLAB_ROWS = D_FF - (D_FF_PAD - DOWN_SLAB)


def _up_col_start(j, offset=0):
    tile = jnp.minimum(j * (UP_TN // LANES), (D_FF - UP_TN) // LANES)
    return (offset // LANES + tile) * LANES


def _down_slab(j, i):
    return j * DOWN_SLABS_PER_COL_TILE + i // DOWN_SLAB_EVERY


def _ffn_up_kernel(h_ref, wg_ref, wv_ref, cg_ref, cv_ref, wd_ref, o_ref, wdo_ref,
                   wg_bf, wv_bf, cw_ref, ug_ref, uv_ref):
    tm, tn = UP_TM, UP_TN
    j, i = pl.program_id(0), pl.program_id(1)
    last = UP_COL_TILES - 1

    def stage_weights(shift):
        keep = tn - shift
        _cast_weight_tile(wg_ref.at[0], wg_bf, shift)
        _cast_weight_tile(wv_ref.at[0], wv_bf, shift)
        for r, ref in enumerate((cg_ref, cv_ref)):
            rows = slice(r * SUBLANES, r * SUBLANES + FFN_KERNEL)
            cw_ref[rows, 0:keep] = ref[0, :, shift:tn]
            if shift:
                cw_ref[rows, keep:tn] = jnp.zeros((FFN_KERNEL, shift), F32)

    @pl.when((i == 0) & (j != last))
    def _():
        stage_weights(0)

    @pl.when((i == 0) & (j == last))
    def _():
        stage_weights(UP_LAST_SHIFT)

    def stage_down(rows):
        wdo_ref[0:rows, :] = wd_ref[0:rows, :].astype(BF16)
        if rows < DOWN_SLAB:
            wdo_ref[rows:DOWN_SLAB, :] = jnp.zeros((DOWN_SLAB - rows, D_MODEL), BF16)

    is_visit = i % DOWN_SLAB_EVERY == 0
    is_last_slab = (j == last) & (i // DOWN_SLAB_EVERY == DOWN_SLABS_PER_COL_TILE - 1)

    @pl.when(is_visit & jnp.logical_not(is_last_slab))
    def _():
        stage_down(DOWN_SLAB)

    @pl.when(is_visit & is_last_slab)
    def _():
        stage_down(DOWN_LAST_SLAB_ROWS)

    @pl.when(i % UP_TILES_PER_SEQ == 0)
    def _():
        for u_ref in (ug_ref, uv_ref):
            u_ref[:, tm:tm + UP_TAIL, :] = jnp.zeros((tn // LANES, UP_TAIL, LANES), F32)

    for u_ref in (ug_ref, uv_ref):
        u_ref[:, 0:UP_TAIL, :] = u_ref[:, tm:tm + UP_TAIL, :]

    h = h_ref[...]
    for u_ref, w_bf in ((ug_ref, wg_bf), (uv_ref, wv_bf)):
        up = jnp.dot(h, w_bf[...], preferred_element_type=F32)
        for c in range(tn // LANES):
            u_ref[c, UP_TAIL:UP_TAIL + tm, :] = up[:, c * LANES:(c + 1) * LANES]

    def conv(u_ref, w_row, c):
        acc = None
        for k in range(FFN_KERNEL):
            first = UP_TAIL - (FFN_KERNEL - 1 - k)
            term = (cw_ref[w_row + k:w_row + k + 1, c * LANES:(c + 1) * LANES]
                    * u_ref[c, first:first + tm, :])
            acc = term if acc is None else acc + term
        return acc

    for c in range(tn // LANES):
        gt = conv(ug_ref, 0, c)
        vl = conv(uv_ref, SUBLANES, c)
        o_ref[:, c * LANES:(c + 1) * LANES] = (gt * _sigmoid(gt) * vl).astype(o_ref.dtype)


def _ffn_up(h, ffn_up, ffn_conv_w, ffn_down, layer):
    tm, tn = UP_TM, UP_TN
    w_spec = lambda off: pl.BlockSpec((pl.Element(1), pl.Element(D_MODEL), pl.Element(tn)),
                                      lambda j, i: (layer, 0, _up_col_start(j, off)))
    cw_spec = lambda off: pl.BlockSpec((pl.Element(1), pl.Element(FFN_KERNEL), pl.Element(tn)),
                                       lambda j, i: (layer, 0, _up_col_start(j, off)))
    u_scratch = pltpu.VMEM((tn // LANES, UP_TAIL + tm, LANES), F32)
    est = (2 * tm * D_MODEL * 2 + 2 * 2 * D_MODEL * tn * 4 + 2 * D_MODEL * tn * 2 + 2 * tm * tn * 2
           + 2 * (tm + UP_TAIL) * tn * 4 + 2 * DOWN_SLAB * D_MODEL * (4 + 2)
           + 4 * tm * tn * 4)
    return pl.pallas_call(
        _ffn_up_kernel,
        grid=(UP_COL_TILES, UP_ROW_TILES),
        in_specs=[pl.BlockSpec((tm, D_MODEL), lambda j, i: (i, 0)),
                  w_spec(0), w_spec(D_FF), cw_spec(0), cw_spec(D_FF),
                  pl.BlockSpec((None, DOWN_SLAB, D_MODEL),
                               lambda j, i: (layer, _down_slab(j, i), 0))],
        out_specs=[pl.BlockSpec((tm, tn), lambda j, i: (i, j)),
                   pl.BlockSpec((DOWN_SLAB, D_MODEL), lambda j, i: (_down_slab(j, i), 0))],
        out_shape=[jax.ShapeDtypeStruct((ROWS, D_FF_PAD), BF16),
                   jax.ShapeDtypeStruct((D_FF_PAD, D_MODEL), BF16)],
        scratch_shapes=[pltpu.VMEM((D_MODEL, tn), BF16), pltpu.VMEM((D_MODEL, tn), BF16),
                        pltpu.VMEM((2 * SUBLANES, tn), F32), u_scratch, u_scratch],
        compiler_params=_compiler_params(("arbitrary", "arbitrary"), est),
        name="ffn_up",
    )(h, ffn_up, ffn_up, ffn_conv_w, ffn_conv_w, ffn_down)


DOWN_TM = 512


def _ffn_down_kernel(act_ref, w_ref, x_ref, g_ref, *out_refs, last):
    xn = x_ref[...] + jnp.dot(act_ref[...], w_ref[...], preferred_element_type=F32)
    normed = _rms_scale(xn, g_ref[...])
    if last:
        (y_ref,) = out_refs
        y_ref[...] = normed
    else:
        xo_ref, ho_ref = out_refs
        xo_ref[...] = xn
        ho_ref[...] = normed.astype(ho_ref.dtype)


def _ffn_down(act, wd, x, g, last):
    tm = DOWN_TM
    row_spec = pl.BlockSpec((tm, D_MODEL), lambda i: (i, 0))
    if last:
        out_specs = [row_spec]
        out_shape = [jax.ShapeDtypeStruct((ROWS, D_MODEL), F32)]
    else:
        out_specs = [row_spec, row_spec]
        out_shape = [jax.ShapeDtypeStruct((ROWS, D_MODEL), F32),
                     jax.ShapeDtypeStruct((ROWS, D_MODEL), BF16)]
    est = (2 * tm * D_FF_PAD * 2 + D_FF_PAD * D_MODEL * 2 + 2 * tm * D_MODEL * (4 + 4 + 2)
           + 3 * tm * D_MODEL * 4)
    return pl.pallas_call(
        functools.partial(_ffn_down_kernel, last=last),
        grid=(ROWS // tm,),
        in_specs=[pl.BlockSpec((tm, D_FF_PAD), lambda i: (i, 0)),
                  pl.BlockSpec((D_FF_PAD, D_MODEL), lambda i: (0, 0), pipeline_mode=pl.Buffered(1)),
                  row_spec,
                  pl.BlockSpec((1, D_MODEL), lambda i: (0, 0))],
        out_specs=out_specs,
        out_shape=out_shape,
        compiler_params=_compiler_params(("arbitrary",), est),
        name="ffn_down",
    )(act, wd, x, g.reshape(1, -1))


def kernel(x, norm1_g, w_in, gate_b, pool_w, pool_scale, pool_proj, conf_conv_w, conf_conv_b,
           conf_ln_g, conf_ln_b, conf_proj, sconv_w, sconv_proj, w_o, norm2_g, ffn_up, ffn_conv_w,
           ffn_down, final_g):
    assert x.shape == (BATCH, SEQ, D_MODEL) and w_in.shape == (DEPTH, D_MODEL, D_IN)
    xs = x.reshape(ROWS, D_MODEL)
    wa, wb, wc, wo = (w.astype(BF16) for w in (pool_proj, conf_proj, sconv_proj, w_o))
    h = _rmsnorm_cast(xs, norm1_g, 0)
    for l in range(DEPTH):
        z, zg, conv_y, conv_q = _inproj(h, w_in, conf_conv_w, conf_conv_b, sconv_w, l)
        xs, h = _tokenmix(z, zg, conv_y, conv_q, xs, l, pool_w, pool_scale, conf_ln_g, conf_ln_b,
                          gate_b, wa, wb, wc, wo, norm2_g)
        act, wd = _ffn_up(h, ffn_up, ffn_conv_w, ffn_down, l)
        last = l == DEPTH - 1
        g_next = final_g if last else norm1_g[l + 1]
        outs = _ffn_down(act, wd, xs, g_next, last)
        if last:
            (y,) = outs
        else:
            xs, h = outs
    return y.reshape(BATCH, SEQ, D_MODEL)
```
